```python
import jax, jax.numpy as jnp
from jax import lax
import numpy as np

D_MODEL = 1024
BATCH = 16
SEQ = 256
DEPTH = 4
DEC_BATCH = 8
DEC_SEQ = 4096
PAST_LEN = 256

GRID_W = 64
N_MIXERS = 3
N_HEADS = 16
N_KV_HEADS = 4
HEAD_DIM = 64
GROUP = N_HEADS // N_KV_HEADS
ATTN_WIDTH = N_HEADS * HEAD_DIM
KV_WIDTH = N_KV_HEADS * HEAD_DIM
IN_WIDTH = 2 * ATTN_WIDTH + 2 * KV_WIDTH
Q_BLOCK = 128
WINDOW = 128
NA_MAX_ROWS = 8
NA_COLS = 16
ROPE_BASE = 10000.0
ROPE_PAIRS = HEAD_DIM // 4
EPS = 1e-6
NEG_INF = -1e30
N_B_LAYERS = (DEPTH + 1) // N_MIXERS
N_C_LAYERS = DEPTH // N_MIXERS

kernel_name = 'hybrid_dit_interleaved_attention_step'


def rms_norm(x, g):
    xf = x.astype(jnp.float32)
    y = xf * lax.rsqrt(jnp.mean(xf * xf, axis=-1, keepdims=True) + EPS)
    return (y * g.astype(jnp.float32)).astype(x.dtype)


def rope_tables(n):
    t = jnp.arange(n, dtype=jnp.int32)
    row = (t // GRID_W).astype(jnp.float32)
    col = (t % GRID_W).astype(jnp.float32)
    inv = ROPE_BASE ** (-jnp.arange(ROPE_PAIRS, dtype=jnp.float32) / ROPE_PAIRS)
    ang = jnp.concatenate([row[:, None] * inv, col[:, None] * inv], axis=-1)
    return jnp.cos(ang), jnp.sin(ang)


def apply_rope(x, cos, sin):
    half = HEAD_DIM // 2
    xf = x.astype(jnp.float32)
    x1, x2 = xf[..., :half], xf[..., half:]
    c_, s_ = cos[None, :, None, :], sin[None, :, None, :]
    return jnp.concatenate([x1 * c_ - x2 * s_, x2 * c_ + x1 * s_], axis=-1).astype(x.dtype)


def modulation(cond, w_mod_l, b_mod_l):
    m = jax.nn.silu(cond) @ w_mod_l + b_mod_l
    return jnp.split(m, 3, axis=-1)


def project(u, w_in_l, q_g, k_g):
    b_, t_ = u.shape[:2]
    p = u @ w_in_l
    q, k, v, z = jnp.split(p, [ATTN_WIDTH, ATTN_WIDTH + KV_WIDTH, ATTN_WIDTH + 2 * KV_WIDTH], axis=-1)
    q = rms_norm(q.reshape(b_, t_, N_HEADS, HEAD_DIM), q_g)
    k = rms_norm(k.reshape(b_, t_, N_KV_HEADS, HEAD_DIM), k_g)
    v = v.reshape(b_, t_, N_KV_HEADS, HEAD_DIM)
    return q, k, v, z


def softmax_with_sink(s, sink):
    if sink is None:
        return jax.nn.softmax(s, axis=-1)
    sk = sink.astype(jnp.float32)[None, :, :, None, None]
    m = jnp.maximum(jnp.max(s, axis=-1, keepdims=True), sk)
    p = jnp.exp(s - m)
    return p / (jnp.sum(p, axis=-1, keepdims=True) + jnp.exp(sk - m))


def to_blocks(q):
    b_, t_ = q.shape[:2]
    return q.reshape(b_, t_ // Q_BLOCK, Q_BLOCK, N_KV_HEADS, GROUP, HEAD_DIM).transpose(1, 0, 2, 3, 4, 5)


def from_blocks(o):
    nb, b_ = o.shape[:2]
    return o.transpose(1, 0, 2, 3, 4, 5).reshape(b_, nb * Q_BLOCK, ATTN_WIDTH)


def dense_attention(q, k, v, sink):
    scale = HEAD_DIM ** -0.5

    def block(qb):
        s = jnp.einsum('bqhgd,bkhd->bhgqk', qb * scale, k, preferred_element_type=jnp.float32)
        p = softmax_with_sink(s, sink).astype(v.dtype)
        return jnp.einsum('bhgqk,bkhd->bqhgd', p, v)

    return from_blocks(lax.map(block, to_blocks(q)))


def window_attention(q, k, v, k_ctx, v_ctx, sink):
    scale = HEAD_DIM ** -0.5
    n = q.shape[1]
    pad = ((0, 0), (WINDOW, WINDOW), (0, 0), (0, 0))
    kp, vp = jnp.pad(k, pad), jnp.pad(v, pad)
    band = Q_BLOCK + 2 * WINDOW
    kofs = jnp.arange(band) - WINDOW
    rel = kofs[None, :] - jnp.arange(Q_BLOCK)[:, None]

    def block(args):
        b, qb = args
        start = b * Q_BLOCK
        kb = lax.dynamic_slice_in_dim(kp, start, band, axis=1)
        vb = lax.dynamic_slice_in_dim(vp, start, band, axis=1)
        kpos = start + kofs
        valid = (jnp.abs(rel) <= WINDOW) & ((kpos >= 0) & (kpos < n))[None, :]
        qs = qb * scale
        s_loc = jnp.einsum('bqhgd,bkhd->bhgqk', qs, kb, preferred_element_type=jnp.float32)
        s_loc = jnp.where(valid, s_loc, NEG_INF)
        s_ctx = jnp.einsum('bqhgd,bkhd->bhgqk', qs, k_ctx, preferred_element_type=jnp.float32)
        p = softmax_with_sink(jnp.concatenate([s_loc, s_ctx], axis=-1), sink).astype(v.dtype)
        return (jnp.einsum('bhgqk,bkhd->bqhgd', p[..., :band], vb)
                + jnp.einsum('bhgqk,bkhd->bqhgd', p[..., band:], v_ctx))

    nb = n // Q_BLOCK
    return from_blocks(lax.map(block, (jnp.arange(nb), to_blocks(q))))


def neighborhood_attention(q, k, v, k_ctx, v_ctx, bias_table):
    scale = HEAD_DIM ** -0.5
    n = q.shape[1]
    rows = n // GRID_W
    wr = min(NA_MAX_ROWS, rows)
    wc = NA_COLS
    n_nb = wr * wc
    kr = jnp.arange(wr)
    kc = jnp.arange(wc)
    bias_h = bias_table.reshape(N_KV_HEADS, GROUP, 2 * NA_MAX_ROWS - 1, 2 * NA_COLS - 1)

    def block(args):
        b, qb = args
        t = b * Q_BLOCK + jnp.arange(Q_BLOCK)
        r, c_ = t // GRID_W, t % GRID_W
        rs = jnp.clip(r - wr // 2, 0, rows - wr)
        cs = jnp.clip(c_ - wc // 2, 0, GRID_W - wc)
        key_r = jnp.broadcast_to(rs[:, None, None] + kr[None, :, None], (Q_BLOCK, wr, wc))
        key_c = jnp.broadcast_to(cs[:, None, None] + kc[None, None, :], (Q_BLOCK, wr, wc))
        idx = (key_r * GRID_W + key_c).reshape(Q_BLOCK, n_nb)
        dr = (key_r - r[:, None, None] + NA_MAX_ROWS - 1).reshape(Q_BLOCK, n_nb)
        dc = (key_c - c_[:, None, None] + NA_COLS - 1).reshape(Q_BLOCK, n_nb)
        bias = bias_h[:, :, dr, dc].astype(jnp.float32)
        kb = jnp.take(k, idx, axis=1)
        vb = jnp.take(v, idx, axis=1)
        qs = qb * scale
        s_loc = jnp.einsum('bqhgd,bqkhd->bhgqk', qs, kb, preferred_element_type=jnp.float32) + bias[None]
        s_ctx = jnp.einsum('bqhgd,bkhd->bhgqk', qs, k_ctx, preferred_element_type=jnp.float32)
        p = jax.nn.softmax(jnp.concatenate([s_loc, s_ctx], axis=-1), axis=-1).astype(v.dtype)
        return (jnp.einsum('bhgqk,bqkhd->bqhgd', p[..., :n_nb], vb)
                + jnp.einsum('bhgqk,bkhd->bqhgd', p[..., n_nb:], v_ctx))

    nb = n // Q_BLOCK
    return from_blocks(lax.map(block, (jnp.arange(nb), to_blocks(q))))


def branch_out(o, z, w_out_l):
    return (o * jax.nn.silu(z)) @ w_out_l


def setup_inputs(seed: int = 0) -> dict:
    key = jax.random.key(seed)
    ks = jax.random.split(key, 17)
    f32 = jnp.float32
    nrm = lambda k_, shape: jax.random.normal(k_, shape, dtype=f32)
    return {
        'x_prompt': nrm(ks[0], (BATCH, SEQ, D_MODEL)),
        'x_sample': nrm(ks[1], (DEC_BATCH, DEC_SEQ, D_MODEL)),
        'cache_k': nrm(ks[2], (DEC_BATCH, DEPTH, PAST_LEN, N_KV_HEADS, HEAD_DIM)),
        'cache_v': nrm(ks[3], (DEC_BATCH, DEPTH, PAST_LEN, N_KV_HEADS, HEAD_DIM)),
        'c': nrm(ks[4], (DEC_BATCH, D_MODEL)),
        'c_ctx': nrm(ks[5], (D_MODEL,)),
        'w_mod': nrm(ks[6], (DEPTH, D_MODEL, 3 * D_MODEL)) * (0.5 * D_MODEL ** -0.5),
        'b_mod': nrm(ks[7], (DEPTH, 3 * D_MODEL)) * 0.02,
        'norm_pre': 1.0 + 0.05 * nrm(ks[8], (DEPTH, D_MODEL)),
        'norm_post': 1.0 + 0.05 * nrm(ks[9], (DEPTH, D_MODEL)),
        'w_in': nrm(ks[10], (DEPTH, D_MODEL, IN_WIDTH)) * D_MODEL ** -0.5,
        'q_norm': 1.0 + 0.05 * nrm(ks[11], (DEPTH, HEAD_DIM)),
        'k_norm': 1.0 + 0.05 * nrm(ks[12], (DEPTH, HEAD_DIM)),
        'w_out': nrm(ks[13], (DEPTH, ATTN_WIDTH, D_MODEL)) * ATTN_WIDTH ** -0.5,
        'sink_logit': 0.5 * nrm(ks[14], (N_B_LAYERS, N_HEADS)),
        'na_rel_bias': 0.1 * nrm(ks[15], (N_C_LAYERS, N_HEADS, 2 * NA_MAX_ROWS - 1, 2 * NA_COLS - 1)),
    }


def reference(x_prompt, x_sample, cache_k, cache_v, c, c_ctx, w_mod, b_mod, norm_pre, norm_post,
              w_in, q_norm, k_norm, w_out, sink_logit, na_rel_bias):
    n_lat = x_sample.shape[1]
    cos, sin = rope_tables(n_lat)
    h_ctx, h_lat = x_prompt, x_sample
    new_k, new_v = [], []
    for l in range(DEPTH):
        kind = l % N_MIXERS
        sink = sink_logit[l // N_MIXERS].reshape(N_KV_HEADS, GROUP) if kind == 1 else None

        shift, scl, gate = modulation(c_ctx, w_mod[l], b_mod[l])
        u = rms_norm(h_ctx, norm_pre[l]) * (1 + scl) + shift
        q, k, v, z = project(u, w_in[l], q_norm[l], k_norm[l])
        new_k.append(k)
        new_v.append(v)
        o = dense_attention(q, k, v, sink)
        h_ctx = h_ctx + gate * rms_norm(branch_out(o, z, w_out[l]), norm_post[l])

        shift, scl, gate = modulation(c, w_mod[l], b_mod[l])
        u = rms_norm(h_lat, norm_pre[l]) * (1 + scl[:, None, :]) + shift[:, None, :]
        q, k, v, z = project(u, w_in[l], q_norm[l], k_norm[l])
        k_ctx, v_ctx = cache_k[:, l], cache_v[:, l]
        if kind == 0:
            q, k = apply_rope(q, cos, sin), apply_rope(k, cos, sin)
            o = dense_attention(q, jnp.concatenate([k, k_ctx], axis=1),
                                jnp.concatenate([v, v_ctx], axis=1), None)
        elif kind == 1:
            q, k = apply_rope(q, cos, sin), apply_rope(k, cos, sin)
            o = window_attention(q, k, v, k_ctx, v_ctx, sink)
        else:
            o = neighborhood_attention(q, k, v, k_ctx, v_ctx, na_rel_bias[l // N_MIXERS])
        h_lat = h_lat + gate[:, None, :] * rms_norm(branch_out(o, z, w_out[l]), norm_post[l])

    new_cache_k = jnp.stack(new_k, axis=1)
    new_cache_v = jnp.stack(new_v, axis=1)
    return (h_ctx, h_lat, new_cache_k, new_cache_v)
```

```python
import functools

import jax
import jax.numpy as jnp
import numpy as np
from jax import lax
from jax.experimental import pallas as pl
from jax.experimental.pallas import tpu as pltpu

D_MODEL = 1024
N_HEADS = 16
N_KV_HEADS = 4
HEAD_DIM = 64
GROUP = N_HEADS // N_KV_HEADS
ATTN_WIDTH = N_HEADS * HEAD_DIM
KV_WIDTH = N_KV_HEADS * HEAD_DIM
QK_WIDTH = ATTN_WIDTH + KV_WIDTH
IN_WIDTH = 2 * ATTN_WIDTH + 2 * KV_WIDTH
GRID_W = 64
N_MIXERS = 3
WINDOW = 128
NA_MAX_ROWS = 8
NA_COLS = 16
ROPE_BASE = 10000.0
ROPE_PAIRS = HEAD_DIM // 4
EPS = 1e-6
NEG_INF = -1e30

LANES = 128
COND_ROWS = 16
Q_TILE = 256
KEY_TILE = 256
TOKEN_TILE = 256
VMEM_LIMIT = 48 * 1024 * 1024

BF16 = jnp.bfloat16
F32 = jnp.float32


def _silu(x):
    return x * (1.0 / (1.0 + jnp.exp(-x)))


def _mod_kernel(cond_ref, w_ref, b_ref, o_ref):
    a = _silu(cond_ref[...])
    o_ref[0] = jnp.dot(a, w_ref[0], preferred_element_type=F32,
                       precision=lax.Precision.HIGHEST) + b_ref[0]


def _modulation(cond, w_mod, b_mod):
    depth = w_mod.shape[0]
    n_col = 3 * D_MODEL // D_MODEL
    return pl.pallas_call(
        _mod_kernel,
        grid=(depth, n_col),
        in_specs=[
            pl.BlockSpec((COND_ROWS, D_MODEL), lambda l, n: (0, 0)),
            pl.BlockSpec((1, D_MODEL, D_MODEL), lambda l, n: (l, 0, n)),
            pl.BlockSpec((1, 1, D_MODEL), lambda l, n: (l, 0, n)),
        ],
        out_specs=pl.BlockSpec((1, COND_ROWS, D_MODEL), lambda l, n: (l, 0, n)),
        out_shape=jax.ShapeDtypeStruct((depth, COND_ROWS, 3 * D_MODEL), F32),
        compiler_params=pltpu.CompilerParams(vmem_limit_bytes=VMEM_LIMIT),
        name="modulation",
    )(cond, w_mod, b_mod.reshape(depth, 1, 3 * D_MODEL))


def _head_norm_rope_t(xt, gain, cos, sin):
    n_heads = xt.shape[0] // HEAD_DIM
    t = xt.shape[1]
    x3 = xt.reshape(n_heads, HEAD_DIM, t)
    ms = jnp.sum(x3 * x3, axis=1, keepdims=True) * (1.0 / HEAD_DIM)
    y = x3 * lax.rsqrt(ms + EPS) * gain[None]
    if cos is not None:
        half = HEAD_DIM // 2
        y1, y2 = y[:, :half], y[:, half:]
        c_, s_ = cos[None], sin[None]
        y = jnp.concatenate([y1 * c_ - y2 * s_, y2 * c_ + y1 * s_], axis=1)
    return y.reshape(n_heads * HEAD_DIM, t)


def _pre_kernel(*refs, rope, emit_kv):
    h_ref, mod_ref, gpre_ref, w_ref, qg_ref, kg_ref = refs[:6]
    pos = 6
    if rope:
        cos_ref, sin_ref = refs[pos:pos + 2]
        pos += 2
    qt_ref, k_ref, vt_ref, gz_ref = refs[pos:pos + 4]
    pos += 4
    if emit_kv:
        kf_ref, vf_ref = refs[pos:pos + 2]

    x = h_ref[0]
    shift = mod_ref[0, 0:1, :]
    scl = mod_ref[0, 1:2, :]
    ms = jnp.mean(x * x, axis=-1, keepdims=True)
    u = (x * lax.rsqrt(ms + EPS) * gpre_ref[...]) * (1.0 + scl) + shift
    p = jnp.dot(u.astype(BF16), w_ref[...], preferred_element_type=F32)

    z = p[:, QK_WIDTH + KV_WIDTH:]
    gz_ref[0] = _silu(z).astype(gz_ref.dtype)
    v = p[:, QK_WIDTH:QK_WIDTH + KV_WIDTH]
    vt_ref[0] = v.T.astype(BF16)

    cos = cos_ref[...] if rope else None
    sin = sin_ref[...] if rope else None
    qt = _head_norm_rope_t(p[:, :ATTN_WIDTH].T, qg_ref[...], cos, sin)
    qt_ref[0] = qt.astype(BF16)
    kt = _head_norm_rope_t(p[:, ATTN_WIDTH:QK_WIDTH].T, kg_ref[...], cos, sin)
    k = kt.T
    k_ref[0] = k.astype(BF16)
    if emit_kv:
        kf_ref[0] = k
        vf_ref[0] = v


def _pre(h, mod, g_pre, w_in, qg, kg, cos_t, sin_t, *, shared_mod, emit_kv):
    b, t, _ = h.shape
    tt = TOKEN_TILE
    rope = cos_t is not None
    mod_map = (lambda bi, ti: (0, 0, 0)) if shared_mod else (lambda bi, ti: (bi, 0, 0))
    in_specs = [
        pl.BlockSpec((1, tt, D_MODEL), lambda bi, ti: (bi, ti, 0)),
        pl.BlockSpec((1, 3, D_MODEL), mod_map),
        pl.BlockSpec((1, D_MODEL), lambda bi, ti: (0, 0)),
        pl.BlockSpec((D_MODEL, IN_WIDTH), lambda bi, ti: (0, 0)),
        pl.BlockSpec((HEAD_DIM, tt), lambda bi, ti: (0, 0)),
        pl.BlockSpec((HEAD_DIM, tt), lambda bi, ti: (0, 0)),
    ]
    args = [h, mod, g_pre, w_in, qg, kg]
    if rope:
        in_specs += [pl.BlockSpec((HEAD_DIM // 2, tt), lambda bi, ti: (0, ti))] * 2
        args += [cos_t, sin_t]
    out_specs = [
        pl.BlockSpec((1, ATTN_WIDTH, tt), lambda bi, ti: (bi, 0, ti)),
        pl.BlockSpec((1, tt, KV_WIDTH), lambda bi, ti: (bi, ti, 0)),
        pl.BlockSpec((1, KV_WIDTH, tt), lambda bi, ti: (bi, 0, ti)),
        pl.BlockSpec((1, tt, ATTN_WIDTH), lambda bi, ti: (bi, ti, 0)),
    ]
    out_shape = [
        jax.ShapeDtypeStruct((b, ATTN_WIDTH, t), BF16),
        jax.ShapeDtypeStruct((b, t, KV_WIDTH), BF16),
        jax.ShapeDtypeStruct((b, KV_WIDTH, t), BF16),
        jax.ShapeDtypeStruct((b, t, ATTN_WIDTH), F32),
    ]
    if emit_kv:
        out_specs += [pl.BlockSpec((1, tt, KV_WIDTH), lambda bi, ti: (bi, ti, 0))] * 2
        out_shape += [jax.ShapeDtypeStruct((b, t, KV_WIDTH), F32)] * 2
    return pl.pallas_call(
        functools.partial(_pre_kernel, rope=rope, emit_kv=emit_kv),
        grid=(b, t // tt),
        in_specs=in_specs,
        out_specs=out_specs,
        out_shape=out_shape,
        compiler_params=pltpu.CompilerParams(vmem_limit_bytes=VMEM_LIMIT),
        name="pre_rope" if rope else "pre_plain",
    )(*args)


def _padded_q(qt, kv_parity):
    zero = jnp.zeros_like(qt)
    lo = jnp.where(kv_parity == 0, qt, zero)
    hi = jnp.where(kv_parity == 0, zero, qt)
    return jnp.concatenate([lo, hi], axis=0)


def _flash_block(qpad, k_blk, vt_blk, bias, m, l, acc):
    s = jnp.dot(k_blk, qpad, preferred_element_type=F32)
    if bias is not None:
        s = s + bias
    m_new = jnp.maximum(m, jnp.max(s, axis=0, keepdims=True))
    alpha = jnp.exp(m - m_new)
    p = jnp.exp(s - m_new)
    l = alpha * l + jnp.sum(p, axis=0, keepdims=True)
    acc = alpha * acc + jnp.dot(vt_blk, p.astype(BF16), preferred_element_type=F32)
    return m_new, l, acc


def _softmax_init(sink_ref, head, tq):
    if sink_ref is None:
        return jnp.full((1, tq), NEG_INF, F32), jnp.zeros((1, tq), F32)
    return jnp.full((1, tq), sink_ref[head], F32), jnp.ones((1, tq), F32)


def _dense_attn_kernel(*refs, n_key_blocks, has_sink):
    if has_sink:
        sink_ref, refs = refs[0], refs[1:]
    else:
        sink_ref = None
    qt_ref, k_ref, vt_ref, gz_ref, o_ref, ot_scr = refs
    kv = pl.program_id(1)
    parity = kv % 2
    tq = qt_ref.shape[2]
    for hh in range(GROUP):
        rows = slice(hh * HEAD_DIM, (hh + 1) * HEAD_DIM)
        qpad = _padded_q(qt_ref[0, rows, :], parity)
        m0, l0 = _softmax_init(sink_ref, kv * GROUP + hh, tq)
        acc0 = jnp.zeros((HEAD_DIM, tq), F32)

        def body(kb, carry):
            start = pl.multiple_of(kb * KEY_TILE, KEY_TILE)
            k_blk = k_ref[0, pl.ds(start, KEY_TILE), :]
            vt_blk = vt_ref[0, :, pl.ds(start, KEY_TILE)]
            return _flash_block(qpad, k_blk, vt_blk, None, *carry)

        m, l, acc = lax.fori_loop(0, n_key_blocks, body, (m0, l0, acc0))
        ot_scr[rows, :] = acc / l
    o_ref[0] = (ot_scr[...].T * gz_ref[0]).astype(o_ref.dtype)


def _dense_attention(qt, k_all, vt_all, gz, sink):
    b, _, t = qt.shape
    tk = k_all.shape[1]
    tq = Q_TILE
    has_sink = sink is not None
    in_specs = [
        pl.BlockSpec((1, GROUP * HEAD_DIM, tq), lambda bi, j, i: (bi, j, i)),
        pl.BlockSpec((1, tk, LANES), lambda bi, j, i: (bi, 0, j // 2)),
        pl.BlockSpec((1, HEAD_DIM, tk), lambda bi, j, i: (bi, j, 0)),
        pl.BlockSpec((1, tq, GROUP * HEAD_DIM), lambda bi, j, i: (bi, i, j)),
    ]
    args = [qt, k_all, vt_all, gz]
    if has_sink:
        in_specs = [pl.BlockSpec(memory_space=pltpu.SMEM)] + in_specs
        args = [sink] + args
    return pl.pallas_call(
        functools.partial(_dense_attn_kernel, n_key_blocks=tk // KEY_TILE, has_sink=has_sink),
        grid=(b, N_KV_HEADS, t // tq),
        in_specs=in_specs,
        out_specs=pl.BlockSpec((1, tq, GROUP * HEAD_DIM), lambda bi, j, i: (bi, i, j)),
        out_shape=jax.ShapeDtypeStruct((b, t, ATTN_WIDTH), BF16),
        scratch_shapes=[pltpu.VMEM((GROUP * HEAD_DIM, tq), F32)],
        compiler_params=pltpu.CompilerParams(vmem_limit_bytes=VMEM_LIMIT),
        name="dense_attn_sink" if has_sink else "dense_attn",
    )(*args)


def _local_attn_kernel(*refs, has_sink, table_heads):
    if has_sink:
        sink_ref, refs = refs[0], refs[1:]
    else:
        sink_ref = None
    (qt_ref, kp_ref, kc_ref, kn_ref, kx_ref, vp_ref, vc_ref, vn_ref, vx_ref,
     tab_ref, gz_ref, o_ref, ot_scr) = refs
    kv = pl.program_id(0)
    parity = kv % 2
    tq = qt_ref.shape[2]
    for hh in range(GROUP):
        rows = slice(hh * HEAD_DIM, (hh + 1) * HEAD_DIM)
        qpad = _padded_q(qt_ref[0, rows, :], parity)
        m, l = _softmax_init(sink_ref, kv * GROUP + hh, tq)
        acc = jnp.zeros((HEAD_DIM, tq), F32)
        th = hh if table_heads > 1 else 0
        m, l, acc = _flash_block(qpad, kx_ref[0], vx_ref[0], None, m, l, acc)
        for n, (k_ref, vt_ref) in enumerate(((kp_ref, vp_ref), (kc_ref, vc_ref), (kn_ref, vn_ref))):
            bias = tab_ref[0, th, n * Q_TILE:(n + 1) * Q_TILE, :]
            m, l, acc = _flash_block(qpad, k_ref[0], vt_ref[0], bias, m, l, acc)
        ot_scr[rows, :] = acc / l
    o_ref[0] = (ot_scr[...].T * gz_ref[0]).astype(o_ref.dtype)


def _local_attention(qt, k, vt, k_ctx, vt_ctx, gz, table, sink):
    b, _, t = qt.shape
    tq = Q_TILE
    nt = t // tq
    has_sink = sink is not None
    table_heads = table.shape[1]
    tab_blk_heads = GROUP if table_heads > 1 else 1

    def cls(i):
        return jnp.where(i == 0, 0, jnp.where(i == nt - 1, 2, 1))

    def prev(i):
        return jnp.maximum(i - 1, 0)

    def nxt(i):
        return jnp.minimum(i + 1, nt - 1)

    in_specs = [
        pl.BlockSpec((1, GROUP * HEAD_DIM, tq), lambda j, i, bi: (bi, j, i)),
        pl.BlockSpec((1, tq, LANES), lambda j, i, bi: (bi, prev(i), j // 2)),
        pl.BlockSpec((1, tq, LANES), lambda j, i, bi: (bi, i, j // 2)),
        pl.BlockSpec((1, tq, LANES), lambda j, i, bi: (bi, nxt(i), j // 2)),
        pl.BlockSpec((1, k_ctx.shape[1], LANES), lambda j, i, bi: (bi, 0, j // 2)),
        pl.BlockSpec((1, HEAD_DIM, tq), lambda j, i, bi: (bi, j, prev(i))),
        pl.BlockSpec((1, HEAD_DIM, tq), lambda j, i, bi: (bi, j, i)),
        pl.BlockSpec((1, HEAD_DIM, tq), lambda j, i, bi: (bi, j, nxt(i))),
        pl.BlockSpec((1, HEAD_DIM, vt_ctx.shape[2]), lambda j, i, bi: (bi, j, 0)),
        pl.BlockSpec((1, tab_blk_heads, 3 * tq, tq),
                     lambda j, i, bi: (cls(i), j if table_heads > 1 else 0, 0, 0)),
        pl.BlockSpec((1, tq, GROUP * HEAD_DIM), lambda j, i, bi: (bi, i, j)),
    ]
    args = [qt, k, k, k, k_ctx, vt, vt, vt, vt_ctx, table, gz]
    if has_sink:
        in_specs = [pl.BlockSpec(memory_space=pltpu.SMEM)] + in_specs
        args = [sink] + args
    return pl.pallas_call(
        functools.partial(_local_attn_kernel, has_sink=has_sink, table_heads=table_heads),
        grid=(N_KV_HEADS, nt, b),
        in_specs=in_specs,
        out_specs=pl.BlockSpec((1, tq, GROUP * HEAD_DIM), lambda j, i, bi: (bi, i, j)),
        out_shape=jax.ShapeDtypeStruct((b, t, ATTN_WIDTH), BF16),
        scratch_shapes=[pltpu.VMEM((GROUP * HEAD_DIM, tq), F32)],
        compiler_params=pltpu.CompilerParams(vmem_limit_bytes=VMEM_LIMIT),
        name="local_attn_sink" if has_sink else "local_attn",
    )(*args)


def _post_kernel(og_ref, w_ref, h_ref, mod_ref, gpost_ref, o_ref):
    y = jnp.dot(og_ref[0], w_ref[...], preferred_element_type=F32)
    ms = jnp.mean(y * y, axis=-1, keepdims=True)
    yn = y * lax.rsqrt(ms + EPS) * gpost_ref[...]
    o_ref[0] = h_ref[0] + mod_ref[0, 2:3, :] * yn


def _post(og, w_out, h, mod, g_post, *, shared_mod):
    b, t, _ = h.shape
    tt = TOKEN_TILE
    mod_map = (lambda bi, ti: (0, 0, 0)) if shared_mod else (lambda bi, ti: (bi, 0, 0))
    return pl.pallas_call(
        _post_kernel,
        grid=(b, t // tt),
        in_specs=[
            pl.BlockSpec((1, tt, ATTN_WIDTH), lambda bi, ti: (bi, ti, 0)),
            pl.BlockSpec((ATTN_WIDTH, D_MODEL), lambda bi, ti: (0, 0)),
            pl.BlockSpec((1, tt, D_MODEL), lambda bi, ti: (bi, ti, 0)),
            pl.BlockSpec((1, 3, D_MODEL), mod_map),
            pl.BlockSpec((1, D_MODEL), lambda bi, ti: (0, 0)),
        ],
        out_specs=pl.BlockSpec((1, tt, D_MODEL), lambda bi, ti: (bi, ti, 0)),
        out_shape=jax.ShapeDtypeStruct((b, t, D_MODEL), F32),
        compiler_params=pltpu.CompilerParams(vmem_limit_bytes=VMEM_LIMIT),
        name="post",
    )(og, w_out, h, mod, g_post)


def _rope_tables_t(n):
    t = jnp.arange(n, dtype=jnp.int32)
    row = (t // GRID_W).astype(F32)
    col = (t % GRID_W).astype(F32)
    inv = ROPE_BASE ** (-jnp.arange(ROPE_PAIRS, dtype=F32) / ROPE_PAIRS)
    ang = jnp.concatenate([row[:, None] * inv, col[:, None] * inv], axis=-1)
    return jnp.cos(ang).T, jnp.sin(ang).T


def _tile_positions(n, cls):
    nt = n // Q_TILE
    assert nt >= 3
    tile = {0: 0, 1: 1, 2: nt - 1}[cls]
    q_pos = tile * Q_TILE + np.arange(Q_TILE)
    k_pos = (tile - 1) * Q_TILE + np.arange(3 * Q_TILE)
    in_range = (k_pos >= 0) & (k_pos < n)
    return q_pos, k_pos, in_range


def _window_table(n):
    assert WINDOW <= Q_TILE
    tabs = []
    for cls in range(3):
        q_pos, k_pos, in_range = _tile_positions(n, cls)
        valid = (np.abs(k_pos[:, None] - q_pos[None, :]) <= WINDOW) & in_range[:, None]
        tabs.append(np.where(valid, 0.0, NEG_INF).astype(np.float32)[None])
    return jnp.asarray(np.stack(tabs))


def _neighborhood_table(n, bias_table):
    rows = n // GRID_W
    wr = min(NA_MAX_ROWS, rows)
    wc = NA_COLS
    tabs = []
    for cls in range(3):
        q_pos, k_pos, in_range = _tile_positions(n, cls)
        k_pos = np.clip(k_pos, 0, n - 1)
        r, c_ = q_pos // GRID_W, q_pos % GRID_W
        kr, kc = k_pos // GRID_W, k_pos % GRID_W
        rs = np.clip(r - wr // 2, 0, rows - wr)
        cs = np.clip(c_ - wc // 2, 0, GRID_W - wc)
        valid = ((kr[:, None] >= rs[None, :]) & (kr[:, None] < rs[None, :] + wr)
                 & (kc[:, None] >= cs[None, :]) & (kc[:, None] < cs[None, :] + wc)
                 & in_range[:, None])
        assert (valid.sum(axis=0) == wr * wc).all()
        dr = np.clip(kr[:, None] - r[None, :] + NA_MAX_ROWS - 1, 0, 2 * NA_MAX_ROWS - 2)
        dc = np.clip(kc[:, None] - c_[None, :] + NA_COLS - 1, 0, 2 * NA_COLS - 2)
        bias = bias_table[:, dr, dc].astype(F32)
        tabs.append(jnp.where(valid[None], bias, NEG_INF))
    return jnp.stack(tabs)


def kernel(x_prompt, x_sample, cache_k, cache_v, c, c_ctx, w_mod, b_mod, norm_pre, norm_post,
           w_in, q_norm, k_norm, w_out, sink_logit, na_rel_bias):
    depth = w_mod.shape[0]
    n_lat = x_sample.shape[1]
    dec_batch = x_sample.shape[0]
    assert dec_batch + 1 <= COND_ROWS
    assert n_lat % Q_TILE == 0 and x_prompt.shape[1] % Q_TILE == 0 and cache_k.shape[2] % KEY_TILE == 0

    cond = jnp.zeros((COND_ROWS, D_MODEL), F32).at[:dec_batch].set(c).at[dec_batch].set(c_ctx)
    mod_all = _modulation(cond, w_mod, b_mod).reshape(depth, COND_ROWS, 3, D_MODEL)

    cos_t, sin_t = _rope_tables_t(n_lat)
    scale = HEAD_DIM ** -0.5
    w_in_b = w_in.astype(BF16)
    w_out_b = w_out.astype(BF16)
    k_ctx_all = cache_k.reshape(dec_batch, depth, -1, KV_WIDTH).astype(BF16)
    vt_ctx_all = cache_v.reshape(dec_batch, depth, -1, KV_WIDTH).transpose(0, 1, 3, 2).astype(BF16)
    win_table = _window_table(n_lat)

    h_ctx, h_lat = x_prompt, x_sample
    new_k, new_v = [], []
    for l in range(depth):
        kind = l % N_MIXERS
        sink = sink_logit[l // N_MIXERS] if kind == 1 else None
        g_pre = norm_pre[l][None]
        g_post = norm_post[l][None]
        qg = jnp.broadcast_to((q_norm[l] * scale)[:, None], (HEAD_DIM, TOKEN_TILE))
        kg = jnp.broadcast_to(k_norm[l][:, None], (HEAD_DIM, TOKEN_TILE))
        mod_ctx = mod_all[l, dec_batch:dec_batch + 1]
        mod_lat = mod_all[l, :dec_batch]

        qt, k, vt, gz, kf, vf = _pre(h_ctx, mod_ctx, g_pre, w_in_b[l], qg, kg, None, None,
                                     shared_mod=True, emit_kv=True)
        new_k.append(kf.reshape(kf.shape[0], kf.shape[1], N_KV_HEADS, HEAD_DIM))
        new_v.append(vf.reshape(vf.shape[0], vf.shape[1], N_KV_HEADS, HEAD_DIM))
        og = _dense_attention(qt, k, vt, gz, sink)
        h_ctx = _post(og, w_out_b[l], h_ctx, mod_ctx, g_post, shared_mod=True)

        use_rope = kind != 2
        qt, k, vt, gz = _pre(h_lat, mod_lat, g_pre, w_in_b[l], qg, kg,
                             cos_t if use_rope else None, sin_t if use_rope else None,
                             shared_mod=False, emit_kv=False)
        k_ctx, vt_ctx = k_ctx_all[:, l], vt_ctx_all[:, l]
        if kind == 0:
            og = _dense_attention(qt, jnp.concatenate([k, k_ctx], axis=1),
                                  jnp.concatenate([vt, vt_ctx], axis=2), gz, None)
        elif kind == 1:
            og = _local_attention(qt, k, vt, k_ctx, vt_ctx, gz, win_table, sink)
        else:
            table = _neighborhood_table(n_lat, na_rel_bias[l // N_MIXERS])
            og = _local_attention(qt, k, vt, k_ctx, vt_ctx, gz, table, None)
        h_lat = _post(og, w_out_b[l], h_lat, mod_lat, g_post, shared_mod=False)

    return (h_ctx, h_lat, jnp.stack(new_k, axis=1), jnp.stack(new_v, axis=1))
```

```python
import functools

import jax
import jax.numpy as jnp
import numpy as np
from jax import lax
from jax.experimental import pallas as pl
from jax.experimental.pallas import tpu as pltpu

D_MODEL = 1024
N_HEADS = 16
N_KV_HEADS = 4
HEAD_DIM = 64
GROUP = N_HEADS // N_KV_HEADS
ATTN_WIDTH = N_HEADS * HEAD_DIM
KV_WIDTH = N_KV_HEADS * HEAD_DIM
QK_WIDTH = ATTN_WIDTH + KV_WIDTH
IN_WIDTH = 2 * ATTN_WIDTH + 2 * KV_WIDTH
GRID_W = 64
N_MIXERS = 3
WINDOW = 128
NA_MAX_ROWS = 8
NA_COLS = 16
ROPE_BASE = 10000.0
ROPE_PAIRS = HEAD_DIM // 4
EPS = 1e-6
NEG_INF = -1e30

LANES = 128
COND_ROWS = 16
Q_TILE = 256
KEY_TILE = 256
SUM_ROWS = 16
LOG2E = 1.4426950408889634
TOKEN_TILE = 256
VMEM_LIMIT = 48 * 1024 * 1024

BF16 = jnp.bfloat16
F32 = jnp.float32


def _silu(x):
    return x * (1.0 / (1.0 + jnp.exp(-x)))


def _mod_kernel(cond_ref, w_ref, b_ref, o_ref):
    a = _silu(cond_ref[...])
    o_ref[0] = jnp.dot(a, w_ref[0], preferred_element_type=F32,
                       precision=lax.Precision.HIGHEST) + b_ref[0]


def _modulation(cond, w_mod, b_mod):
    depth = w_mod.shape[0]
    n_col = 3 * D_MODEL // D_MODEL
    return pl.pallas_call(
        _mod_kernel,
        grid=(depth, n_col),
        in_specs=[
            pl.BlockSpec((COND_ROWS, D_MODEL), lambda l, n: (0, 0)),
            pl.BlockSpec((1, D_MODEL, D_MODEL), lambda l, n: (l, 0, n)),
            pl.BlockSpec((1, 1, D_MODEL), lambda l, n: (l, 0, n)),
        ],
        out_specs=pl.BlockSpec((1, COND_ROWS, D_MODEL), lambda l, n: (l, 0, n)),
        out_shape=jax.ShapeDtypeStruct((depth, COND_ROWS, 3 * D_MODEL), F32),
        compiler_params=pltpu.CompilerParams(vmem_limit_bytes=VMEM_LIMIT),
        name="modulation",
    )(cond, w_mod, b_mod.reshape(depth, 1, 3 * D_MODEL))


def _head_norm_rope_t(xt, gain, cos, sin):
    n_heads = xt.shape[0] // HEAD_DIM
    t = xt.shape[1]
    x3 = xt.reshape(n_heads, HEAD_DIM, t)
    ms = jnp.sum(x3 * x3, axis=1, keepdims=True) * (1.0 / HEAD_DIM)
    y = x3 * lax.rsqrt(ms + EPS) * gain[None]
    if cos is not None:
        half = HEAD_DIM // 2
        y1, y2 = y[:, :half], y[:, half:]
        c_, s_ = cos[None], sin[None]
        y = jnp.concatenate([y1 * c_ - y2 * s_, y2 * c_ + y1 * s_], axis=1)
    return y.reshape(n_heads * HEAD_DIM, t)


def _pre_kernel(*refs, rope, emit_kv):
    h_ref, mod_ref, gpre_ref, w_ref, qg_ref, kg_ref = refs[:6]
    pos = 6
    if rope:
        cos_ref, sin_ref = refs[pos:pos + 2]
        pos += 2
    qt_ref, k_ref, vt_ref, gz_ref = refs[pos:pos + 4]
    pos += 4
    if emit_kv:
        kf_ref, vf_ref = refs[pos:pos + 2]

    x = h_ref[0]
    shift = mod_ref[0, 0:1, :]
    scl = mod_ref[0, 1:2, :]
    ms = jnp.mean(x * x, axis=-1, keepdims=True)
    u = (x * lax.rsqrt(ms + EPS) * gpre_ref[...]) * (1.0 + scl) + shift
    p = jnp.dot(u.astype(BF16), w_ref[...], preferred_element_type=F32)

    z = p[:, QK_WIDTH + KV_WIDTH:]
    gz_ref[0] = _silu(z).astype(gz_ref.dtype)
    v = p[:, QK_WIDTH:QK_WIDTH + KV_WIDTH]
    vt_ref[0] = v.T.astype(BF16)

    cos = cos_ref[...] if rope else None
    sin = sin_ref[...] if rope else None
    qt = _head_norm_rope_t(p[:, :ATTN_WIDTH].T, qg_ref[...], cos, sin)
    qt_ref[0] = qt.astype(BF16)
    kt = _head_norm_rope_t(p[:, ATTN_WIDTH:QK_WIDTH].T, kg_ref[...], cos, sin)
    k = kt.T
    k_ref[0] = k.astype(BF16)
    if emit_kv:
        kf_ref[0] = k
        vf_ref[0] = v


def _pre(h, mod, g_pre, w_in, qg, kg, cos_t, sin_t, *, shared_mod, emit_kv):
    b, t, _ = h.shape
    tt = TOKEN_TILE
    rope = cos_t is not None
    mod_map = (lambda bi, ti: (0, 0, 0)) if shared_mod else (lambda bi, ti: (bi, 0, 0))
    in_specs = [
        pl.BlockSpec((1, tt, D_MODEL), lambda bi, ti: (bi, ti, 0)),
        pl.BlockSpec((1, 3, D_MODEL), mod_map),
        pl.BlockSpec((1, D_MODEL), lambda bi, ti: (0, 0)),
        pl.BlockSpec((D_MODEL, IN_WIDTH), lambda bi, ti: (0, 0)),
        pl.BlockSpec((HEAD_DIM, tt), lambda bi, ti: (0, 0)),
        pl.BlockSpec((HEAD_DIM, tt), lambda bi, ti: (0, 0)),
    ]
    args = [h, mod, g_pre, w_in, qg, kg]
    if rope:
        in_specs += [pl.BlockSpec((HEAD_DIM // 2, tt), lambda bi, ti: (0, ti))] * 2
        args += [cos_t, sin_t]
    out_specs = [
        pl.BlockSpec((1, ATTN_WIDTH, tt), lambda bi, ti: (bi, 0, ti)),
        pl.BlockSpec((1, tt, KV_WIDTH), lambda bi, ti: (bi, ti, 0)),
        pl.BlockSpec((1, KV_WIDTH, tt), lambda bi, ti: (bi, 0, ti)),
        pl.BlockSpec((1, tt, ATTN_WIDTH), lambda bi, ti: (bi, ti, 0)),
    ]
    out_shape = [
        jax.ShapeDtypeStruct((b, ATTN_WIDTH, t), BF16),
        jax.ShapeDtypeStruct((b, t, KV_WIDTH), BF16),
        jax.ShapeDtypeStruct((b, KV_WIDTH, t), BF16),
        jax.ShapeDtypeStruct((b, t, ATTN_WIDTH), F32),
    ]
    if emit_kv:
        out_specs += [pl.BlockSpec((1, tt, KV_WIDTH), lambda bi, ti: (bi, ti, 0))] * 2
        out_shape += [jax.ShapeDtypeStruct((b, t, KV_WIDTH), F32)] * 2
    return pl.pallas_call(
        functools.partial(_pre_kernel, rope=rope, emit_kv=emit_kv),
        grid=(b, t // tt),
        in_specs=in_specs,
        out_specs=out_specs,
        out_shape=out_shape,
        compiler_params=pltpu.CompilerParams(vmem_limit_bytes=VMEM_LIMIT),
        name="pre_rope" if rope else "pre_plain",
    )(*args)


def _padded_q(qt, kv_parity):
    zero = jnp.zeros_like(qt)
    lo = jnp.where(kv_parity == 0, qt, zero)
    hi = jnp.where(kv_parity == 0, zero, qt)
    return jnp.concatenate([lo, hi], axis=0)


class _GroupFlash:
    def __init__(self, qt_ref, s_scr, kv, sink_ref):
        parity = kv % 2
        tq = qt_ref.shape[2]
        self.s_scr = s_scr
        self.qpads = [_padded_q(qt_ref[0, hh * HEAD_DIM:(hh + 1) * HEAD_DIM, :], parity)
                      for hh in range(GROUP)]
        self.ones = jnp.ones((SUM_ROWS, KEY_TILE), BF16)
        self.stats = []
        for hh in range(GROUP):
            acc = jnp.zeros((HEAD_DIM + SUM_ROWS, tq), F32)
            if sink_ref is None:
                m = jnp.full((1, tq), NEG_INF, F32)
            else:
                m = jnp.full((1, tq), sink_ref[kv * GROUP + hh] * LOG2E, F32)
                row = lax.broadcasted_iota(jnp.int32, acc.shape, 0)
                acc = jnp.where(row == HEAD_DIM, 1.0, acc)
            self.stats.append((m, acc))

    def score(self, slot, k_blk):
        for hh in range(GROUP):
            self.s_scr[slot, hh] = jnp.dot(k_blk, self.qpads[hh], preferred_element_type=F32)

    def consume(self, slot, vt_blk, bias=None):
        vt_aug = jnp.concatenate([vt_blk, self.ones], axis=0)
        for hh in range(GROUP):
            m, acc = self.stats[hh]
            s = self.s_scr[slot, hh]
            if bias is not None:
                s = s + bias(hh)
            m_new = jnp.maximum(m, jnp.max(s, axis=0, keepdims=True))
            alpha = jnp.exp2(m - m_new)
            p = jnp.exp2(s - m_new).astype(BF16)
            acc = alpha * acc + jnp.dot(vt_aug, p, preferred_element_type=F32)
            self.stats[hh] = (m_new, acc)

    def finish(self, ot_scr):
        for hh in range(GROUP):
            _, acc = self.stats[hh]
            ot_scr[hh * HEAD_DIM:(hh + 1) * HEAD_DIM, :] = acc[:HEAD_DIM] / acc[HEAD_DIM:HEAD_DIM + 1]


def _dense_attn_kernel(*refs, n_key_blocks, has_sink):
    if has_sink:
        sink_ref, refs = refs[0], refs[1:]
    else:
        sink_ref = None
    qt_ref, k_ref, vt_ref, gz_ref, o_ref, ot_scr, s_scr = refs
    flash = _GroupFlash(qt_ref, s_scr, pl.program_id(1), sink_ref)

    def key_range(kb):
        start = kb * KEY_TILE
        if not isinstance(kb, int):
            start = pl.multiple_of(start, KEY_TILE)
        return pl.ds(start, KEY_TILE)

    def k_block(kb):
        return k_ref[0, key_range(kb), :]

    def vt_block(kb):
        return vt_ref[0, :, key_range(kb)]

    flash.score(0, k_block(0))
    n_pairs = (n_key_blocks - 1) // 2

    def body(i, stats):
        flash.stats = list(stats)
        flash.score(1, k_block(2 * i + 1))
        flash.consume(0, vt_block(2 * i))
        flash.score(0, k_block(2 * i + 2))
        flash.consume(1, vt_block(2 * i + 1))
        return tuple(flash.stats)

    if n_pairs > 0:
        flash.stats = list(lax.fori_loop(0, n_pairs, body, tuple(flash.stats)))
    if n_key_blocks - 2 * n_pairs == 2:
        flash.score(1, k_block(n_key_blocks - 1))
        flash.consume(0, vt_block(n_key_blocks - 2))
        flash.consume(1, vt_block(n_key_blocks - 1))
    else:
        flash.consume(0, vt_block(n_key_blocks - 1))
    flash.finish(ot_scr)
    o_ref[0] = (ot_scr[...].T * gz_ref[0]).astype(o_ref.dtype)


def _dense_attention(qt, k_all, vt_all, gz, sink):
    b, _, t = qt.shape
    tk = k_all.shape[1]
    tq = Q_TILE
    has_sink = sink is not None
    in_specs = [
        pl.BlockSpec((1, GROUP * HEAD_DIM, tq), lambda bi, j, i: (bi, j, i)),
        pl.BlockSpec((1, tk, LANES), lambda bi, j, i: (bi, 0, j // 2)),
        pl.BlockSpec((1, HEAD_DIM, tk), lambda bi, j, i: (bi, j, 0)),
        pl.BlockSpec((1, tq, GROUP * HEAD_DIM), lambda bi, j, i: (bi, i, j)),
    ]
    args = [qt, k_all, vt_all, gz]
    if has_sink:
        in_specs = [pl.BlockSpec(memory_space=pltpu.SMEM)] + in_specs
        args = [sink] + args
    return pl.pallas_call(
        functools.partial(_dense_attn_kernel, n_key_blocks=tk // KEY_TILE, has_sink=has_sink),
        grid=(b, N_KV_HEADS, t // tq),
        in_specs=in_specs,
        out_specs=pl.BlockSpec((1, tq, GROUP * HEAD_DIM), lambda bi, j, i: (bi, i, j)),
        out_shape=jax.ShapeDtypeStruct((b, t, ATTN_WIDTH), BF16),
        scratch_shapes=[pltpu.VMEM((GROUP * HEAD_DIM, tq), F32),
                        pltpu.VMEM((2, GROUP, KEY_TILE, tq), F32)],
        compiler_params=pltpu.CompilerParams(vmem_limit_bytes=VMEM_LIMIT),
        name="dense_attn_sink" if has_sink else "dense_attn",
    )(*args)


def _local_attn_kernel(*refs, has_sink, table_heads):
    if has_sink:
        sink_ref, refs = refs[0], refs[1:]
    else:
        sink_ref = None
    (qt_ref, kp_ref, kc_ref, kn_ref, kx_ref, vp_ref, vc_ref, vn_ref, vx_ref,
     tab_ref, gz_ref, o_ref, ot_scr, s_scr) = refs
    flash = _GroupFlash(qt_ref, s_scr, pl.program_id(0), sink_ref)

    def bias(n):
        return lambda hh: tab_ref[0, hh if table_heads > 1 else 0, n * Q_TILE:(n + 1) * Q_TILE, :]

    flash.score(0, kx_ref[0])
    flash.score(1, kp_ref[0])
    flash.consume(0, vx_ref[0])
    flash.score(0, kc_ref[0])
    flash.consume(1, vp_ref[0], bias(0))
    flash.score(1, kn_ref[0])
    flash.consume(0, vc_ref[0], bias(1))
    flash.consume(1, vn_ref[0], bias(2))
    flash.finish(ot_scr)
    o_ref[0] = (ot_scr[...].T * gz_ref[0]).astype(o_ref.dtype)


def _local_attention(qt, k, vt, k_ctx, vt_ctx, gz, table, sink):
    b, _, t = qt.shape
    tq = Q_TILE
    nt = t // tq
    has_sink = sink is not None
    table_heads = table.shape[1]
    tab_blk_heads = GROUP if table_heads > 1 else 1

    def cls(i):
        return jnp.where(i == 0, 0, jnp.where(i == nt - 1, 2, 1))

    def prev(i):
        return jnp.maximum(i - 1, 0)

    def nxt(i):
        return jnp.minimum(i + 1, nt - 1)

    in_specs = [
        pl.BlockSpec((1, GROUP * HEAD_DIM, tq), lambda j, i, bi: (bi, j, i)),
        pl.BlockSpec((1, tq, LANES), lambda j, i, bi: (bi, prev(i), j // 2)),
        pl.BlockSpec((1, tq, LANES), lambda j, i, bi: (bi, i, j // 2)),
        pl.BlockSpec((1, tq, LANES), lambda j, i, bi: (bi, nxt(i), j // 2)),
        pl.BlockSpec((1, k_ctx.shape[1], LANES), lambda j, i, bi: (bi, 0, j // 2)),
        pl.BlockSpec((1, HEAD_DIM, tq), lambda j, i, bi: (bi, j, prev(i))),
        pl.BlockSpec((1, HEAD_DIM, tq), lambda j, i, bi: (bi, j, i)),
        pl.BlockSpec((1, HEAD_DIM, tq), lambda j, i, bi: (bi, j, nxt(i))),
        pl.BlockSpec((1, HEAD_DIM, vt_ctx.shape[2]), lambda j, i, bi: (bi, j, 0)),
        pl.BlockSpec((1, tab_blk_heads, 3 * tq, tq),
                     lambda j, i, bi: (cls(i), j if table_heads > 1 else 0, 0, 0)),
        pl.BlockSpec((1, tq, GROUP * HEAD_DIM), lambda j, i, bi: (bi, i, j)),
    ]
    args = [qt, k, k, k, k_ctx, vt, vt, vt, vt_ctx, table, gz]
    if has_sink:
        in_specs = [pl.BlockSpec(memory_space=pltpu.SMEM)] + in_specs
        args = [sink] + args
    return pl.pallas_call(
        functools.partial(_local_attn_kernel, has_sink=has_sink, table_heads=table_heads),
        grid=(N_KV_HEADS, nt, b),
        in_specs=in_specs,
        out_specs=pl.BlockSpec((1, tq, GROUP * HEAD_DIM), lambda j, i, bi: (bi, i, j)),
        out_shape=jax.ShapeDtypeStruct((b, t, ATTN_WIDTH), BF16),
        scratch_shapes=[pltpu.VMEM((GROUP * HEAD_DIM, tq), F32),
                        pltpu.VMEM((2, GROUP, KEY_TILE, tq), F32)],
        compiler_params=pltpu.CompilerParams(vmem_limit_bytes=VMEM_LIMIT),
        name="local_attn_sink" if has_sink else "local_attn",
    )(*args)


def _post_kernel(og_ref, w_ref, h_ref, mod_ref, gpost_ref, o_ref):
    y = jnp.dot(og_ref[0], w_ref[...], preferred_element_type=F32)
    ms = jnp.mean(y * y, axis=-1, keepdims=True)
    yn = y * lax.rsqrt(ms + EPS) * gpost_ref[...]
    o_ref[0] = h_ref[0] + mod_ref[0, 2:3, :] * yn


def _post(og, w_out, h, mod, g_post, *, shared_mod):
    b, t, _ = h.shape
    tt = TOKEN_TILE
    mod_map = (lambda bi, ti: (0, 0, 0)) if shared_mod else (lambda bi, ti: (bi, 0, 0))
    return pl.pallas_call(
        _post_kernel,
        grid=(b, t // tt),
        in_specs=[
            pl.BlockSpec((1, tt, ATTN_WIDTH), lambda bi, ti: (bi, ti, 0)),
            pl.BlockSpec((ATTN_WIDTH, D_MODEL), lambda bi, ti: (0, 0)),
            pl.BlockSpec((1, tt, D_MODEL), lambda bi, ti: (bi, ti, 0)),
            pl.BlockSpec((1, 3, D_MODEL), mod_map),
            pl.BlockSpec((1, D_MODEL), lambda bi, ti: (0, 0)),
        ],
        out_specs=pl.BlockSpec((1, tt, D_MODEL), lambda bi, ti: (bi, ti, 0)),
        out_shape=jax.ShapeDtypeStruct((b, t, D_MODEL), F32),
        compiler_params=pltpu.CompilerParams(vmem_limit_bytes=VMEM_LIMIT),
        name="post",
    )(og, w_out, h, mod, g_post)


def _rope_tables_t(n):
    t = jnp.arange(n, dtype=jnp.int32)
    row = (t // GRID_W).astype(F32)
    col = (t % GRID_W).astype(F32)
    inv = ROPE_BASE ** (-jnp.arange(ROPE_PAIRS, dtype=F32) / ROPE_PAIRS)
    ang = jnp.concatenate([row[:, None] * inv, col[:, None] * inv], axis=-1)
    return jnp.cos(ang).T, jnp.sin(ang).T


def _tile_positions(n, cls):
    nt = n // Q_TILE
    assert nt >= 3
    tile = {0: 0, 1: 1, 2: nt - 1}[cls]
    q_pos = tile * Q_TILE + np.arange(Q_TILE)
    k_pos = (tile - 1) * Q_TILE + np.arange(3 * Q_TILE)
    in_range = (k_pos >= 0) & (k_pos < n)
    return q_pos, k_pos, in_range


def _window_table(n):
    assert WINDOW <= Q_TILE
    tabs = []
    for cls in range(3):
        q_pos, k_pos, in_range = _tile_positions(n, cls)
        valid = (np.abs(k_pos[:, None] - q_pos[None, :]) <= WINDOW) & in_range[:, None]
        tabs.append(np.where(valid, 0.0, NEG_INF).astype(np.float32)[None])
    return jnp.asarray(np.stack(tabs))


def _neighborhood_table(n, bias_table):
    rows = n // GRID_W
    wr = min(NA_MAX_ROWS, rows)
    wc = NA_COLS
    assert Q_TILE % GRID_W == 0
    tile_rows = Q_TILE // GRID_W
    dr_rel = (np.arange(3 * tile_rows)[:, None] - tile_rows - np.arange(tile_rows)[None, :]
              + NA_MAX_ROWS - 1)
    dc_rel = np.arange(GRID_W)[:, None] - np.arange(GRID_W)[None, :] + NA_COLS - 1
    oh_r = (dr_rel[..., None] == np.arange(2 * NA_MAX_ROWS - 1)).astype(np.float32)
    oh_c = (dc_rel[..., None] == np.arange(2 * NA_COLS - 1)).astype(np.float32)
    full = jnp.einsum('kqa,hab,cdb->hkcqd', oh_r, bias_table.astype(F32), oh_c,
                      precision=lax.Precision.HIGHEST)
    full = full.reshape(bias_table.shape[0], 3 * Q_TILE, Q_TILE)
    tabs = []
    for cls in range(3):
        q_pos, k_pos, in_range = _tile_positions(n, cls)
        k_pos = np.clip(k_pos, 0, n - 1)
        r, c_ = q_pos // GRID_W, q_pos % GRID_W
        kr, kc = k_pos // GRID_W, k_pos % GRID_W
        rs = np.clip(r - wr // 2, 0, rows - wr)
        cs = np.clip(c_ - wc // 2, 0, GRID_W - wc)
        valid = ((kr[:, None] >= rs[None, :]) & (kr[:, None] < rs[None, :] + wr)
                 & (kc[:, None] >= cs[None, :]) & (kc[:, None] < cs[None, :] + wc)
                 & in_range[:, None])
        assert (valid.sum(axis=0) == wr * wc).all()
        dr = kr[:, None] - r[None, :] + NA_MAX_ROWS - 1
        dc = kc[:, None] - c_[None, :] + NA_COLS - 1
        dr_tab = np.broadcast_to(dr_rel[:, None, :, None], (3 * tile_rows, GRID_W, tile_rows, GRID_W))
        dc_tab = np.broadcast_to(dc_rel[None, :, None, :], (3 * tile_rows, GRID_W, tile_rows, GRID_W))
        assert (dr_tab.reshape(dr.shape)[valid] == dr[valid]).all()
        assert (dc_tab.reshape(dc.shape)[valid] == dc[valid]).all()
        tabs.append(jnp.where(valid[None], full, NEG_INF))
    return jnp.stack(tabs)


def kernel(x_prompt, x_sample, cache_k, cache_v, c, c_ctx, w_mod, b_mod, norm_pre, norm_post,
           w_in, q_norm, k_norm, w_out, sink_logit, na_rel_bias):
    depth = w_mod.shape[0]
    n_lat = x_sample.shape[1]
    dec_batch = x_sample.shape[0]
    assert dec_batch + 1 <= COND_ROWS
    assert n_lat % Q_TILE == 0 and x_prompt.shape[1] % Q_TILE == 0 and cache_k.shape[2] % KEY_TILE == 0

    cond = jnp.zeros((COND_ROWS, D_MODEL), F32).at[:dec_batch].set(c).at[dec_batch].set(c_ctx)
    mod_all = _modulation(cond, w_mod, b_mod).reshape(depth, COND_ROWS, 3, D_MODEL)

    cos_t, sin_t = _rope_tables_t(n_lat)
    scale = HEAD_DIM ** -0.5
    w_in_b = w_in.astype(BF16)
    w_out_b = w_out.astype(BF16)
    k_ctx_all = cache_k.reshape(dec_batch, depth, -1, KV_WIDTH).astype(BF16)
    vt_ctx_all = cache_v.reshape(dec_batch, depth, -1, KV_WIDTH).transpose(0, 1, 3, 2).astype(BF16)
    win_table = _window_table(n_lat)

    h_ctx, h_lat = x_prompt, x_sample
    new_k, new_v = [], []
    for l in range(depth):
        kind = l % N_MIXERS
        sink = sink_logit[l // N_MIXERS] if kind == 1 else None
        g_pre = norm_pre[l][None]
        g_post = norm_post[l][None]
        qg = jnp.broadcast_to((q_norm[l] * (scale * LOG2E))[:, None], (HEAD_DIM, TOKEN_TILE))
        kg = jnp.broadcast_to(k_norm[l][:, None], (HEAD_DIM, TOKEN_TILE))
        mod_ctx = mod_all[l, dec_batch:dec_batch + 1]
        mod_lat = mod_all[l, :dec_batch]

        qt, k, vt, gz, kf, vf = _pre(h_ctx, mod_ctx, g_pre, w_in_b[l], qg, kg, None, None,
                                     shared_mod=True, emit_kv=True)
        new_k.append(kf.reshape(kf.shape[0], kf.shape[1], N_KV_HEADS, HEAD_DIM))
        new_v.append(vf.reshape(vf.shape[0], vf.shape[1], N_KV_HEADS, HEAD_DIM))
        og = _dense_attention(qt, k, vt, gz, sink)
        h_ctx = _post(og, w_out_b[l], h_ctx, mod_ctx, g_post, shared_mod=True)

        use_rope = kind != 2
        qt, k, vt, gz = _pre(h_lat, mod_lat, g_pre, w_in_b[l], qg, kg,
                             cos_t if use_rope else None, sin_t if use_rope else None,
                             shared_mod=False, emit_kv=False)
        k_ctx, vt_ctx = k_ctx_all[:, l], vt_ctx_all[:, l]
        if kind == 0:
            og = _dense_attention(qt, jnp.concatenate([k, k_ctx], axis=1),
                                  jnp.concatenate([vt, vt_ctx], axis=2), gz, None)
        elif kind == 1:
            og = _local_attention(qt, k, vt, k_ctx, vt_ctx, gz, win_table, sink)
        else:
            table = _neighborhood_table(n_lat, na_rel_bias[l // N_MIXERS] * LOG2E)
            og = _local_attention(qt, k, vt, k_ctx, vt_ctx, gz, table, None)
        h_lat = _post(og, w_out_b[l], h_lat, mod_lat, g_post, shared_mod=False)

    return (h_ctx, h_lat, jnp.stack(new_k, axis=1), jnp.stack(new_v, axis=1))
```

```python
import functools

import jax
import jax.numpy as jnp
import numpy as np
from jax import lax
from jax.experimental import pallas as pl
from jax.experimental.pallas import tpu as pltpu

D_MODEL = 1024
N_HEADS = 16
N_KV_HEADS = 4
HEAD_DIM = 64
GROUP = N_HEADS // N_KV_HEADS
ATTN_WIDTH = N_HEADS * HEAD_DIM
KV_WIDTH = N_KV_HEADS * HEAD_DIM
QK_WIDTH = ATTN_WIDTH + KV_WIDTH
IN_WIDTH = 2 * ATTN_WIDTH + 2 * KV_WIDTH
GRID_W = 64
N_MIXERS = 3
WINDOW = 128
NA_MAX_ROWS = 8
NA_COLS = 16
ROPE_BASE = 10000.0
ROPE_PAIRS = HEAD_DIM // 4
EPS = 1e-6
NEG_INF = -1e30

LANES = 128
N_KV_PAIRS = KV_WIDTH // LANES
PAIR_HEADS = 2 * GROUP
PAIR_WIDTH = PAIR_HEADS * HEAD_DIM
COND_ROWS = 16
Q_TILE = 256
KEY_TILE = 256
SUM_ROWS = 16
DENSE_BLOCKS_PER_ITER = 8
DENSE_KV_PER_STEP = 1
LOG2E = 1.4426950408889634
TOKEN_TILE = 256
POST_TILE = 512
VMEM_LIMIT = 48 * 1024 * 1024

BF16 = jnp.bfloat16
F32 = jnp.float32


def _silu(x):
    return x * (1.0 / (1.0 + jnp.exp(-x)))


def _mod_kernel(cond_ref, w_ref, b_ref, o_ref):
    a = _silu(cond_ref[...])
    o_ref[0] = jnp.dot(a, w_ref[0], preferred_element_type=F32,
                       precision=lax.Precision.HIGHEST) + b_ref[0]


def _modulation(cond, w_mod, b_mod):
    depth = w_mod.shape[0]
    n_col = 3 * D_MODEL // D_MODEL
    return pl.pallas_call(
        _mod_kernel,
        grid=(depth, n_col),
        in_specs=[
            pl.BlockSpec((COND_ROWS, D_MODEL), lambda l, n: (0, 0)),
            pl.BlockSpec((1, D_MODEL, D_MODEL), lambda l, n: (l, 0, n)),
            pl.BlockSpec((1, 1, D_MODEL), lambda l, n: (l, 0, n)),
        ],
        out_specs=pl.BlockSpec((1, COND_ROWS, D_MODEL), lambda l, n: (l, 0, n)),
        out_shape=jax.ShapeDtypeStruct((depth, COND_ROWS, 3 * D_MODEL), F32),
        compiler_params=pltpu.CompilerParams(vmem_limit_bytes=VMEM_LIMIT),
        name="modulation",
    )(cond, w_mod, b_mod.reshape(depth, 1, 3 * D_MODEL))


def _head_norm_rope_t(xt, gain, cos, sin):
    n_heads = xt.shape[0] // HEAD_DIM
    t = xt.shape[1]
    x3 = xt.reshape(n_heads, HEAD_DIM, t)
    ms = jnp.sum(x3 * x3, axis=1, keepdims=True) * (1.0 / HEAD_DIM)
    y = x3 * lax.rsqrt(ms + EPS) * gain[None]
    if cos is not None:
        half = HEAD_DIM // 2
        y1, y2 = y[:, :half], y[:, half:]
        c_, s_ = cos[None], sin[None]
        y = jnp.concatenate([y1 * c_ - y2 * s_, y2 * c_ + y1 * s_], axis=1)
    return y.reshape(n_heads * HEAD_DIM, t)


def _pre_kernel(*refs, rope, emit_kv):
    h_ref, mod_ref, gpre_ref, w_ref, qg_ref, kg_ref = refs[:6]
    pos = 6
    if rope:
        cos_ref, sin_ref = refs[pos:pos + 2]
        pos += 2
    qt_ref, k_ref, vt_ref, gz_ref = refs[pos:pos + 4]
    pos += 4
    if emit_kv:
        kf_ref, vf_ref = refs[pos:pos + 2]

    x = h_ref[0]
    shift = mod_ref[0, 0:1, :]
    scl = mod_ref[0, 1:2, :]
    ms = jnp.mean(x * x, axis=-1, keepdims=True)
    u = (x * lax.rsqrt(ms + EPS) * gpre_ref[...]) * (1.0 + scl) + shift
    p = jnp.dot(u.astype(BF16), w_ref[...], preferred_element_type=F32)

    z = p[:, QK_WIDTH + KV_WIDTH:]
    gz_ref[0] = _silu(z).astype(gz_ref.dtype)
    v = p[:, QK_WIDTH:QK_WIDTH + KV_WIDTH]
    vt_ref[0] = v.T.astype(BF16)

    cos = cos_ref[...] if rope else None
    sin = sin_ref[...] if rope else None
    qt = _head_norm_rope_t(p[:, :ATTN_WIDTH].T, qg_ref[...], cos, sin)
    qt_ref[0] = qt.astype(BF16)
    kt = _head_norm_rope_t(p[:, ATTN_WIDTH:QK_WIDTH].T, kg_ref[...], cos, sin)
    k = kt.T
    k_ref[0] = k.astype(BF16)
    if emit_kv:
        kf_ref[0] = k
        vf_ref[0] = v


def _pre(h, mod, g_pre, w_in, qg, kg, cos_t, sin_t, *, shared_mod, emit_kv):
    b, t, _ = h.shape
    tt = TOKEN_TILE
    rope = cos_t is not None
    mod_map = (lambda bi, ti: (0, 0, 0)) if shared_mod else (lambda bi, ti: (bi, 0, 0))
    in_specs = [
        pl.BlockSpec((1, tt, D_MODEL), lambda bi, ti: (bi, ti, 0)),
        pl.BlockSpec((1, 3, D_MODEL), mod_map),
        pl.BlockSpec((1, D_MODEL), lambda bi, ti: (0, 0)),
        pl.BlockSpec((D_MODEL, IN_WIDTH), lambda bi, ti: (0, 0)),
        pl.BlockSpec((HEAD_DIM, tt), lambda bi, ti: (0, 0)),
        pl.BlockSpec((HEAD_DIM, tt), lambda bi, ti: (0, 0)),
    ]
    args = [h, mod, g_pre, w_in, qg, kg]
    if rope:
        in_specs += [pl.BlockSpec((HEAD_DIM // 2, tt), lambda bi, ti: (0, ti))] * 2
        args += [cos_t, sin_t]
    out_specs = [
        pl.BlockSpec((1, ATTN_WIDTH, tt), lambda bi, ti: (bi, 0, ti)),
        pl.BlockSpec((1, tt, KV_WIDTH), lambda bi, ti: (bi, ti, 0)),
        pl.BlockSpec((1, KV_WIDTH, tt), lambda bi, ti: (bi, 0, ti)),
        pl.BlockSpec((1, tt, ATTN_WIDTH), lambda bi, ti: (bi, ti, 0)),
    ]
    out_shape = [
        jax.ShapeDtypeStruct((b, ATTN_WIDTH, t), BF16),
        jax.ShapeDtypeStruct((b, t, KV_WIDTH), BF16),
        jax.ShapeDtypeStruct((b, KV_WIDTH, t), BF16),
        jax.ShapeDtypeStruct((b, t, ATTN_WIDTH), BF16),
    ]
    if emit_kv:
        out_specs += [pl.BlockSpec((1, tt, KV_WIDTH), lambda bi, ti: (bi, ti, 0))] * 2
        out_shape += [jax.ShapeDtypeStruct((b, t, KV_WIDTH), F32)] * 2
    return pl.pallas_call(
        functools.partial(_pre_kernel, rope=rope, emit_kv=emit_kv),
        grid=(b, t // tt),
        in_specs=in_specs,
        out_specs=out_specs,
        out_shape=out_shape,
        compiler_params=pltpu.CompilerParams(vmem_limit_bytes=VMEM_LIMIT),
        name="pre_rope" if rope else "pre_plain",
    )(*args)


def _padded_q(qt, kv_parity):
    zero = jnp.zeros_like(qt)
    if isinstance(kv_parity, int):
        return jnp.concatenate([zero, qt] if kv_parity else [qt, zero], axis=0)
    return jnp.concatenate([jnp.where(kv_parity == 0, qt, zero), jnp.where(kv_parity == 0, zero, qt)], axis=0)


class _HeadsFlash:
    def __init__(self, qt_ref, s_scr, max_scr, j, kv_per_step, sink_ref):
        tq = qt_ref.shape[2]
        self.n_heads = kv_per_step * GROUP
        self.s_scr = s_scr
        self.max_scr = max_scr
        self.qpads = [_padded_q(qt_ref[0, hh * HEAD_DIM:(hh + 1) * HEAD_DIM, :],
                                hh // GROUP if kv_per_step == 2 else j % 2)
                      for hh in range(self.n_heads)]
        self.ones = jnp.ones((SUM_ROWS, KEY_TILE), BF16)
        self.stats = []
        for hh in range(self.n_heads):
            acc = jnp.zeros((HEAD_DIM + SUM_ROWS, tq), F32)
            if sink_ref is None:
                m = jnp.full((1, tq), NEG_INF, F32)
            else:
                m = jnp.full((1, tq), sink_ref[j * self.n_heads + hh] * LOG2E, F32)
                row = lax.broadcasted_iota(jnp.int32, acc.shape, 0)
                acc = jnp.where(row == HEAD_DIM, 1.0, acc)
            self.stats.append((m, acc))

    def score(self, slot, k_blk, bias=None):
        for hh in range(self.n_heads):
            s = jnp.dot(k_blk, self.qpads[hh], preferred_element_type=F32)
            if bias is not None:
                s = s + bias(hh)
            self.s_scr[slot, hh] = s
            self.max_scr[slot, hh] = jnp.max(s, axis=0, keepdims=True)

    def consume(self, slot, vt_blk):
        vt_aug = [jnp.concatenate([vt_blk[kv * HEAD_DIM:(kv + 1) * HEAD_DIM], self.ones], axis=0)
                  for kv in range(self.n_heads // GROUP)]
        for hh in range(self.n_heads):
            m, acc = self.stats[hh]
            m_new = jnp.maximum(m, self.max_scr[slot, hh])
            alpha = jnp.exp2(m - m_new)
            p = jnp.exp2((self.s_scr[slot, hh] - m_new).astype(BF16))
            acc = alpha * acc + jnp.dot(vt_aug[hh // GROUP], p, preferred_element_type=F32)
            self.stats[hh] = (m_new, acc)

    def finish(self, ot_scr):
        for hh in range(self.n_heads):
            _, acc = self.stats[hh]
            ot_scr[hh * HEAD_DIM:(hh + 1) * HEAD_DIM, :] = acc[:HEAD_DIM] / acc[HEAD_DIM:HEAD_DIM + 1]


def _dense_attn_kernel(*refs, n_key_blocks, has_sink):
    if has_sink:
        sink_ref, refs = refs[0], refs[1:]
    else:
        sink_ref = None
    qt_ref, k_ref, vt_ref, gz_ref, o_ref, ot_scr, s_scr, max_scr = refs
    flash = _HeadsFlash(qt_ref, s_scr, max_scr, pl.program_id(1), DENSE_KV_PER_STEP, sink_ref)

    def key_range(kb):
        start = kb * KEY_TILE
        if not isinstance(kb, int):
            start = pl.multiple_of(start, KEY_TILE)
        return pl.ds(start, KEY_TILE)

    def k_block(kb):
        return k_ref[0, key_range(kb), :]

    def vt_block(kb):
        return vt_ref[0, :, key_range(kb)]

    flash.score(0, k_block(0))
    per_iter = DENSE_BLOCKS_PER_ITER
    n_iter = (n_key_blocks - 1) // per_iter

    def run_blocks(first, count, score_last):
        for u in range(count):
            if u + 1 < count or score_last:
                flash.score((u + 1) % 2, k_block(first + u + 1))
            flash.consume(u % 2, vt_block(first + u))

    def body(i, stats):
        flash.stats = list(stats)
        run_blocks(i * per_iter, per_iter, True)
        return tuple(flash.stats)

    if n_iter == 1:
        run_blocks(0, per_iter, True)
    elif n_iter > 1:
        flash.stats = list(lax.fori_loop(0, n_iter, body, tuple(flash.stats)))
    run_blocks(n_iter * per_iter, n_key_blocks - n_iter * per_iter, False)
    flash.finish(ot_scr)
    o_ref[0] = (ot_scr[...].T * gz_ref[0]).astype(o_ref.dtype)


def _dense_attention(qt, k_all, vt_all, gz, sink):
    b, _, t = qt.shape
    tk = k_all.shape[1]
    tq = Q_TILE
    has_sink = sink is not None
    kvs = DENSE_KV_PER_STEP
    heads = kvs * GROUP
    width = heads * HEAD_DIM
    in_specs = [
        pl.BlockSpec((1, width, tq), lambda bi, j, i: (bi, j, i)),
        pl.BlockSpec((1, tk, LANES), lambda bi, j, i: (bi, 0, j * kvs // 2)),
        pl.BlockSpec((1, kvs * HEAD_DIM, tk), lambda bi, j, i: (bi, j, 0)),
        pl.BlockSpec((1, tq, width), lambda bi, j, i: (bi, i, j)),
    ]
    args = [qt, k_all, vt_all, gz]
    if has_sink:
        in_specs = [pl.BlockSpec(memory_space=pltpu.SMEM)] + in_specs
        args = [sink] + args
    return pl.pallas_call(
        functools.partial(_dense_attn_kernel, n_key_blocks=tk // KEY_TILE, has_sink=has_sink),
        grid=(b, N_KV_HEADS // kvs, t // tq),
        in_specs=in_specs,
        out_specs=pl.BlockSpec((1, tq, width), lambda bi, j, i: (bi, i, j)),
        out_shape=jax.ShapeDtypeStruct((b, t, ATTN_WIDTH), BF16),
        scratch_shapes=[pltpu.VMEM((width, tq), F32),
                        pltpu.VMEM((2, heads, KEY_TILE, tq), F32),
                        pltpu.VMEM((2, heads, 1, tq), F32)],
        compiler_params=pltpu.CompilerParams(vmem_limit_bytes=VMEM_LIMIT),
        name="dense_attn_sink" if has_sink else "dense_attn",
    )(*args)


def _local_attn_kernel(*refs, has_sink, table_heads):
    if has_sink:
        sink_ref, refs = refs[0], refs[1:]
    else:
        sink_ref = None
    (qt_ref, kp_ref, kc_ref, kn_ref, kx_ref, vp_ref, vc_ref, vn_ref, vx_ref,
     tab_ref, gz_ref, o_ref, ot_scr, s_scr, max_scr) = refs
    flash = _HeadsFlash(qt_ref, s_scr, max_scr, pl.program_id(0), 2, sink_ref)

    def bias(n):
        return lambda hh: tab_ref[0, hh if table_heads > 1 else 0, n * Q_TILE:(n + 1) * Q_TILE, :]

    flash.score(0, kx_ref[0])
    flash.score(1, kp_ref[0], bias(0))
    flash.consume(0, vx_ref[0])
    flash.score(0, kc_ref[0], bias(1))
    flash.consume(1, vp_ref[0])
    flash.score(1, kn_ref[0], bias(2))
    flash.consume(0, vc_ref[0])
    flash.consume(1, vn_ref[0])
    flash.finish(ot_scr)
    o_ref[0] = (ot_scr[...].T * gz_ref[0]).astype(o_ref.dtype)


def _local_attention(qt, k, vt, k_ctx, vt_ctx, gz, table, sink):
    b, _, t = qt.shape
    tq = Q_TILE
    nt = t // tq
    has_sink = sink is not None
    table_heads = table.shape[1]
    tab_blk_heads = PAIR_HEADS if table_heads > 1 else 1

    def cls(i):
        return jnp.where(i == 0, 0, jnp.where(i == nt - 1, 2, 1))

    def prev(i):
        return jnp.maximum(i - 1, 0)

    def nxt(i):
        return jnp.minimum(i + 1, nt - 1)

    in_specs = [
        pl.BlockSpec((1, PAIR_WIDTH, tq), lambda j, i, bi: (bi, j, i)),
        pl.BlockSpec((1, tq, LANES), lambda j, i, bi: (bi, prev(i), j)),
        pl.BlockSpec((1, tq, LANES), lambda j, i, bi: (bi, i, j)),
        pl.BlockSpec((1, tq, LANES), lambda j, i, bi: (bi, nxt(i), j)),
        pl.BlockSpec((1, k_ctx.shape[1], LANES), lambda j, i, bi: (bi, 0, j)),
        pl.BlockSpec((1, 2 * HEAD_DIM,tq), lambda j, i, bi: (bi, j, prev(i))),
        pl.BlockSpec((1, 2 * HEAD_DIM,tq), lambda j, i, bi: (bi, j, i)),
        pl.BlockSpec((1, 2 * HEAD_DIM,tq), lambda j, i, bi: (bi, j, nxt(i))),
        pl.BlockSpec((1, 2 * HEAD_DIM,vt_ctx.shape[2]), lambda j, i, bi: (bi, j, 0)),
        pl.BlockSpec((1, tab_blk_heads, 3 * tq, tq),
                     lambda j, i, bi: (cls(i), j if table_heads > 1 else 0, 0, 0)),
        pl.BlockSpec((1, tq, PAIR_WIDTH), lambda j, i, bi: (bi, i, j)),
    ]
    args = [qt, k, k, k, k_ctx, vt, vt, vt, vt_ctx, table, gz]
    if has_sink:
        in_specs = [pl.BlockSpec(memory_space=pltpu.SMEM)] + in_specs
        args = [sink] + args
    return pl.pallas_call(
        functools.partial(_local_attn_kernel, has_sink=has_sink, table_heads=table_heads),
        grid=(N_KV_PAIRS, nt, b),
        in_specs=in_specs,
        out_specs=pl.BlockSpec((1, tq, PAIR_WIDTH), lambda j, i, bi: (bi, i, j)),
        out_shape=jax.ShapeDtypeStruct((b, t, ATTN_WIDTH), BF16),
        scratch_shapes=[pltpu.VMEM((PAIR_WIDTH, tq), F32),
                        pltpu.VMEM((2, PAIR_HEADS, KEY_TILE, tq), F32),
                        pltpu.VMEM((2, PAIR_HEADS, 1, tq), F32)],
        compiler_params=pltpu.CompilerParams(vmem_limit_bytes=VMEM_LIMIT),
        name="local_attn_sink" if has_sink else "local_attn",
    )(*args)


def _post_kernel(og_ref, w_ref, h_ref, mod_ref, gpost_ref, o_ref):
    y = jnp.dot(og_ref[0], w_ref[...], preferred_element_type=F32)
    ms = jnp.mean(y * y, axis=-1, keepdims=True)
    yn = y * lax.rsqrt(ms + EPS) * gpost_ref[...]
    o_ref[0] = h_ref[0] + mod_ref[0, 2:3, :] * yn


def _post(og, w_out, h, mod, g_post, *, shared_mod):
    b, t, _ = h.shape
    tt = min(POST_TILE, t)
    mod_map = (lambda bi, ti: (0, 0, 0)) if shared_mod else (lambda bi, ti: (bi, 0, 0))
    return pl.pallas_call(
        _post_kernel,
        grid=(b, t // tt),
        in_specs=[
            pl.BlockSpec((1, tt, ATTN_WIDTH), lambda bi, ti: (bi, ti, 0)),
            pl.BlockSpec((ATTN_WIDTH, D_MODEL), lambda bi, ti: (0, 0)),
            pl.BlockSpec((1, tt, D_MODEL), lambda bi, ti: (bi, ti, 0)),
            pl.BlockSpec((1, 3, D_MODEL), mod_map),
            pl.BlockSpec((1, D_MODEL), lambda bi, ti: (0, 0)),
        ],
        out_specs=pl.BlockSpec((1, tt, D_MODEL), lambda bi, ti: (bi, ti, 0)),
        out_shape=jax.ShapeDtypeStruct((b, t, D_MODEL), F32),
        compiler_params=pltpu.CompilerParams(vmem_limit_bytes=VMEM_LIMIT),
        name="post",
    )(og, w_out, h, mod, g_post)


def _rope_tables_t(n):
    t = jnp.arange(n, dtype=jnp.int32)
    row = (t // GRID_W).astype(F32)
    col = (t % GRID_W).astype(F32)
    inv = ROPE_BASE ** (-jnp.arange(ROPE_PAIRS, dtype=F32) / ROPE_PAIRS)
    ang = jnp.concatenate([row[:, None] * inv, col[:, None] * inv], axis=-1)
    return jnp.cos(ang).T, jnp.sin(ang).T


def _tile_positions(n, cls):
    nt = n // Q_TILE
    assert nt >= 3
    tile = {0: 0, 1: 1, 2: nt - 1}[cls]
    q_pos = tile * Q_TILE + np.arange(Q_TILE)
    k_pos = (tile - 1) * Q_TILE + np.arange(3 * Q_TILE)
    in_range = (k_pos >= 0) & (k_pos < n)
    return q_pos, k_pos, in_range


def _window_table(n):
    assert WINDOW <= Q_TILE
    tabs = []
    for cls in range(3):
        q_pos, k_pos, in_range = _tile_positions(n, cls)
        valid = (np.abs(k_pos[:, None] - q_pos[None, :]) <= WINDOW) & in_range[:, None]
        tabs.append(np.where(valid, 0.0, NEG_INF).astype(np.float32)[None])
    return jnp.asarray(np.stack(tabs))


def _neighborhood_table(n, bias_table):
    rows = n // GRID_W
    wr = min(NA_MAX_ROWS, rows)
    wc = NA_COLS
    assert Q_TILE % GRID_W == 0
    tile_rows = Q_TILE // GRID_W
    dr_rel = (np.arange(3 * tile_rows)[:, None] - tile_rows - np.arange(tile_rows)[None, :]
              + NA_MAX_ROWS - 1)
    dc_rel = np.arange(GRID_W)[:, None] - np.arange(GRID_W)[None, :] + NA_COLS - 1
    oh_r = (dr_rel[..., None] == np.arange(2 * NA_MAX_ROWS - 1)).astype(np.float32)
    oh_c = (dc_rel[..., None] == np.arange(2 * NA_COLS - 1)).astype(np.float32)
    full = jnp.einsum('kqa,hab,cdb->hkcqd', oh_r, bias_table.astype(F32), oh_c,
                      precision=lax.Precision.HIGHEST)
    full = full.reshape(bias_table.shape[0], 3 * Q_TILE, Q_TILE)
    tabs = []
    for cls in range(3):
        q_pos, k_pos, in_range = _tile_positions(n, cls)
        k_pos = np.clip(k_pos, 0, n - 1)
        r, c_ = q_pos // GRID_W, q_pos % GRID_W
        kr, kc = k_pos // GRID_W, k_pos % GRID_W
        rs = np.clip(r - wr // 2, 0, rows - wr)
        cs = np.clip(c_ - wc // 2, 0, GRID_W - wc)
        valid = ((kr[:, None] >= rs[None, :]) & (kr[:, None] < rs[None, :] + wr)
                 & (kc[:, None] >= cs[None, :]) & (kc[:, None] < cs[None, :] + wc)
                 & in_range[:, None])
        assert (valid.sum(axis=0) == wr * wc).all()
        dr = kr[:, None] - r[None, :] + NA_MAX_ROWS - 1
        dc = kc[:, None] - c_[None, :] + NA_COLS - 1
        dr_tab = np.broadcast_to(dr_rel[:, None, :, None], (3 * tile_rows, GRID_W, tile_rows, GRID_W))
        dc_tab = np.broadcast_to(dc_rel[None, :, None, :], (3 * tile_rows, GRID_W, tile_rows, GRID_W))
        assert (dr_tab.reshape(dr.shape)[valid] == dr[valid]).all()
        assert (dc_tab.reshape(dc.shape)[valid] == dc[valid]).all()
        tabs.append(jnp.where(valid[None], full, NEG_INF))
    return jnp.stack(tabs)


def kernel(x_prompt, x_sample, cache_k, cache_v, c, c_ctx, w_mod, b_mod, norm_pre, norm_post,
           w_in, q_norm, k_norm, w_out, sink_logit, na_rel_bias):
    depth = w_mod.shape[0]
    n_lat = x_sample.shape[1]
    dec_batch = x_sample.shape[0]
    assert dec_batch + 1 <= COND_ROWS
    assert n_lat % Q_TILE == 0 and x_prompt.shape[1] % Q_TILE == 0 and cache_k.shape[2] % KEY_TILE == 0

    cond = jnp.zeros((COND_ROWS, D_MODEL), F32).at[:dec_batch].set(c).at[dec_batch].set(c_ctx)
    mod_all = _modulation(cond, w_mod, b_mod).reshape(depth, COND_ROWS, 3, D_MODEL)

    cos_t, sin_t = _rope_tables_t(n_lat)
    scale = HEAD_DIM ** -0.5
    w_in_b = w_in.astype(BF16)
    w_out_b = w_out.astype(BF16)
    k_ctx_all = cache_k.reshape(dec_batch, depth, -1, KV_WIDTH).astype(BF16)
    vt_ctx_all = cache_v.reshape(dec_batch, depth, -1, KV_WIDTH).transpose(0, 1, 3, 2).astype(BF16)
    win_table = _window_table(n_lat)

    h_ctx, h_lat = x_prompt, x_sample
    new_k, new_v = [], []
    for l in range(depth):
        kind = l % N_MIXERS
        sink = sink_logit[l // N_MIXERS] if kind == 1 else None
        g_pre = norm_pre[l][None]
        g_post = norm_post[l][None]
        qg = jnp.broadcast_to((q_norm[l] * (scale * LOG2E))[:, None], (HEAD_DIM, TOKEN_TILE))
        kg = jnp.broadcast_to(k_norm[l][:, None], (HEAD_DIM, TOKEN_TILE))
        mod_ctx = mod_all[l, dec_batch:dec_batch + 1]
        mod_lat = mod_all[l, :dec_batch]

        qt, k, vt, gz, kf, vf = _pre(h_ctx, mod_ctx, g_pre, w_in_b[l], qg, kg, None, None,
                                     shared_mod=True, emit_kv=True)
        new_k.append(kf.reshape(kf.shape[0], kf.shape[1], N_KV_HEADS, HEAD_DIM))
        new_v.append(vf.reshape(vf.shape[0], vf.shape[1], N_KV_HEADS, HEAD_DIM))
        og = _dense_attention(qt, k, vt, gz, sink)
        h_ctx = _post(og, w_out_b[l], h_ctx, mod_ctx, g_post, shared_mod=True)

        use_rope = kind != 2
        qt, k, vt, gz = _pre(h_lat, mod_lat, g_pre, w_in_b[l], qg, kg,
                             cos_t if use_rope else None, sin_t if use_rope else None,
                             shared_mod=False, emit_kv=False)
        k_ctx, vt_ctx = k_ctx_all[:, l], vt_ctx_all[:, l]
        if kind == 0:
            og = _dense_attention(qt, jnp.concatenate([k, k_ctx], axis=1),
                                  jnp.concatenate([vt, vt_ctx], axis=2), gz, None)
        elif kind == 1:
            og = _local_attention(qt, k, vt, k_ctx, vt_ctx, gz, win_table, sink)
        else:
            table = _neighborhood_table(n_lat, na_rel_bias[l // N_MIXERS] * LOG2E)
            og = _local_attention(qt, k, vt, k_ctx, vt_ctx, gz, table, None)
        h_lat = _post(og, w_out_b[l], h_lat, mod_lat, g_post, shared_mod=False)

    return (h_ctx, h_lat, jnp.stack(new_k, axis=1), jnp.stack(new_v, axis=1))
```

```python
import functools

import jax
import jax.numpy as jnp
import numpy as np
from jax import lax
from jax.experimental import pallas as pl
from jax.experimental.pallas import tpu as pltpu

D_MODEL = 1024
N_HEADS = 16
N_KV_HEADS = 4
HEAD_DIM = 64
GROUP = N_HEADS // N_KV_HEADS
ATTN_WIDTH = N_HEADS * HEAD_DIM
KV_WIDTH = N_KV_HEADS * HEAD_DIM
QK_WIDTH = ATTN_WIDTH + KV_WIDTH
IN_WIDTH = 2 * ATTN_WIDTH + 2 * KV_WIDTH
GRID_W = 64
N_MIXERS = 3
WINDOW = 128
NA_MAX_ROWS = 8
NA_COLS = 16
ROPE_BASE = 10000.0
ROPE_PAIRS = HEAD_DIM // 4
EPS = 1e-6
NEG_INF = -1e30

LANES = 128
N_KV_PAIRS = KV_WIDTH // LANES
PAIR_HEADS = 2 * GROUP
PAIR_WIDTH = PAIR_HEADS * HEAD_DIM
COND_ROWS = 16
Q_TILE = 256
KEY_TILE = 256
SUM_ROWS = 16
DENSE_BLOCKS_PER_ITER = 8
LOG2E = 1.4426950408889634
TOKEN_TILE = 256
POST_TILE = 512
VMEM_LIMIT = 48 * 1024 * 1024

BF16 = jnp.bfloat16
F32 = jnp.float32


def _silu(x):
    return x * (1.0 / (1.0 + jnp.exp(-x)))


def _mod_kernel(cond_ref, w_ref, b_ref, o_ref):
    a = _silu(cond_ref[...])
    o_ref[0] = jnp.dot(a, w_ref[0], preferred_element_type=F32,
                       precision=lax.Precision.HIGHEST) + b_ref[0]


def _modulation(cond, w_mod, b_mod):
    depth = w_mod.shape[0]
    n_col = 3 * D_MODEL // D_MODEL
    return pl.pallas_call(
        _mod_kernel,
        grid=(depth, n_col),
        in_specs=[
            pl.BlockSpec((COND_ROWS, D_MODEL), lambda l, n: (0, 0)),
            pl.BlockSpec((1, D_MODEL, D_MODEL), lambda l, n: (l, 0, n)),
            pl.BlockSpec((1, 1, D_MODEL), lambda l, n: (l, 0, n)),
        ],
        out_specs=pl.BlockSpec((1, COND_ROWS, D_MODEL), lambda l, n: (l, 0, n)),
        out_shape=jax.ShapeDtypeStruct((depth, COND_ROWS, 3 * D_MODEL), F32),
        compiler_params=pltpu.CompilerParams(vmem_limit_bytes=VMEM_LIMIT),
        name="modulation",
    )(cond, w_mod, b_mod.reshape(depth, 1, 3 * D_MODEL))


def _head_norm_rope_t(xt, gain, cos, sin):
    n_heads = xt.shape[0] // HEAD_DIM
    t = xt.shape[1]
    x3 = xt.reshape(n_heads, HEAD_DIM, t)
    ms = jnp.sum(x3 * x3, axis=1, keepdims=True) * (1.0 / HEAD_DIM)
    y = x3 * lax.rsqrt(ms + EPS) * gain[None]
    if cos is not None:
        half = HEAD_DIM // 2
        y1, y2 = y[:, :half], y[:, half:]
        c_, s_ = cos[None], sin[None]
        y = jnp.concatenate([y1 * c_ - y2 * s_, y2 * c_ + y1 * s_], axis=1)
    return y.reshape(n_heads * HEAD_DIM, t)


def _pre_kernel(*refs, rope, emit_kv):
    h_ref, mod_ref, gpre_ref, w_ref, qg_ref, kg_ref = refs[:6]
    pos = 6
    if rope:
        cos_ref, sin_ref = refs[pos:pos + 2]
        pos += 2
    qt_ref, k_ref, vt_ref, gz_ref = refs[pos:pos + 4]
    pos += 4
    if emit_kv:
        kf_ref, vf_ref = refs[pos:pos + 2]

    x = h_ref[0]
    shift = mod_ref[0, 0:1, :]
    scl = mod_ref[0, 1:2, :]
    ms = jnp.mean(x * x, axis=-1, keepdims=True)
    u = (x * lax.rsqrt(ms + EPS) * gpre_ref[...]) * (1.0 + scl) + shift
    p = jnp.dot(u.astype(BF16), w_ref[...], preferred_element_type=F32)

    z = p[:, QK_WIDTH + KV_WIDTH:]
    gz_ref[0] = _silu(z).T.astype(gz_ref.dtype)
    v = p[:, QK_WIDTH:QK_WIDTH + KV_WIDTH]
    vt_ref[0] = v.T.astype(BF16)

    cos = cos_ref[...] if rope else None
    sin = sin_ref[...] if rope else None
    qt = _head_norm_rope_t(p[:, :ATTN_WIDTH].T, qg_ref[...], cos, sin)
    qt_ref[0] = qt.astype(BF16)
    kt = _head_norm_rope_t(p[:, ATTN_WIDTH:QK_WIDTH].T, kg_ref[...], cos, sin)
    k = kt.T
    k_ref[0] = k.astype(BF16)
    if emit_kv:
        kf_ref[0] = k
        vf_ref[0] = v


def _pre(h, mod, g_pre, w_in, qg, kg, cos_t, sin_t, *, shared_mod, emit_kv):
    b, t, _ = h.shape
    tt = TOKEN_TILE
    rope = cos_t is not None
    mod_map = (lambda bi, ti: (0, 0, 0)) if shared_mod else (lambda bi, ti: (bi, 0, 0))
    in_specs = [
        pl.BlockSpec((1, tt, D_MODEL), lambda bi, ti: (bi, ti, 0)),
        pl.BlockSpec((1, 3, D_MODEL), mod_map),
        pl.BlockSpec((1, D_MODEL), lambda bi, ti: (0, 0)),
        pl.BlockSpec((D_MODEL, IN_WIDTH), lambda bi, ti: (0, 0)),
        pl.BlockSpec((HEAD_DIM, tt), lambda bi, ti: (0, 0)),
        pl.BlockSpec((HEAD_DIM, tt), lambda bi, ti: (0, 0)),
    ]
    args = [h, mod, g_pre, w_in, qg, kg]
    if rope:
        in_specs += [pl.BlockSpec((HEAD_DIM // 2, tt), lambda bi, ti: (0, ti))] * 2
        args += [cos_t, sin_t]
    out_specs = [
        pl.BlockSpec((1, ATTN_WIDTH, tt), lambda bi, ti: (bi, 0, ti)),
        pl.BlockSpec((1, tt, KV_WIDTH), lambda bi, ti: (bi, ti, 0)),
        pl.BlockSpec((1, KV_WIDTH, tt), lambda bi, ti: (bi, 0, ti)),
        pl.BlockSpec((1, ATTN_WIDTH, tt), lambda bi, ti: (bi, 0, ti)),
    ]
    out_shape = [
        jax.ShapeDtypeStruct((b, ATTN_WIDTH, t), BF16),
        jax.ShapeDtypeStruct((b, t, KV_WIDTH), BF16),
        jax.ShapeDtypeStruct((b, KV_WIDTH, t), BF16),
        jax.ShapeDtypeStruct((b, ATTN_WIDTH, t), BF16),
    ]
    if emit_kv:
        out_specs += [pl.BlockSpec((1, tt, KV_WIDTH), lambda bi, ti: (bi, ti, 0))] * 2
        out_shape += [jax.ShapeDtypeStruct((b, t, KV_WIDTH), F32)] * 2
    return pl.pallas_call(
        functools.partial(_pre_kernel, rope=rope, emit_kv=emit_kv),
        grid=(b, t // tt),
        in_specs=in_specs,
        out_specs=out_specs,
        out_shape=out_shape,
        compiler_params=pltpu.CompilerParams(vmem_limit_bytes=VMEM_LIMIT),
        name="pre_rope" if rope else "pre_plain",
    )(*args)


def _padded_q(qt, kv_parity):
    zero = jnp.zeros_like(qt)
    if isinstance(kv_parity, int):
        return jnp.concatenate([zero, qt] if kv_parity else [qt, zero], axis=0)
    return jnp.concatenate([jnp.where(kv_parity == 0, qt, zero), jnp.where(kv_parity == 0, zero, qt)], axis=0)


class _HeadsFlash:
    def __init__(self, qt_ref, s_scr, max_scr, pair, heads, sink_ref):
        tq = qt_ref.shape[2]
        self.heads = list(heads)
        self.s_scr = s_scr
        self.max_scr = max_scr
        self.qpads = [_padded_q(qt_ref[0, hh * HEAD_DIM:(hh + 1) * HEAD_DIM, :], hh // GROUP)
                      for hh in self.heads]
        self.ones = jnp.ones((SUM_ROWS, KEY_TILE), BF16)
        self.stats = []
        for hh in self.heads:
            acc = jnp.zeros((HEAD_DIM + SUM_ROWS, tq), F32)
            if sink_ref is None:
                m = jnp.full((1, tq), NEG_INF, F32)
            else:
                m = jnp.full((1, tq), sink_ref[pair * PAIR_HEADS + hh] * LOG2E, F32)
                row = lax.broadcasted_iota(jnp.int32, acc.shape, 0)
                acc = jnp.where(row == HEAD_DIM, 1.0, acc)
            self.stats.append((m, acc))

    def score(self, slot, k_blk, bias=None):
        for n, hh in enumerate(self.heads):
            s = jnp.dot(k_blk, self.qpads[n], preferred_element_type=F32)
            if bias is not None:
                s = s + bias(hh)
            self.s_scr[slot, n] = s
            self.max_scr[slot, n] = jnp.max(s, axis=0, keepdims=True)

    def consume(self, slot, vt_blk):
        vt_aug = {kv: jnp.concatenate([vt_blk[kv * HEAD_DIM:(kv + 1) * HEAD_DIM], self.ones], axis=0)
                  for kv in sorted({hh // GROUP for hh in self.heads})}
        for n, hh in enumerate(self.heads):
            m, acc = self.stats[n]
            m_new = jnp.maximum(m, self.max_scr[slot, n])
            alpha = jnp.exp2(m - m_new)
            p = jnp.exp2((self.s_scr[slot, n] - m_new).astype(BF16))
            acc = alpha * acc + jnp.dot(vt_aug[hh // GROUP], p, preferred_element_type=F32)
            self.stats[n] = (m_new, acc)

    def finish(self, gzt_ref, o_ref):
        for n, hh in enumerate(self.heads):
            _, acc = self.stats[n]
            rows = slice(hh * HEAD_DIM, (hh + 1) * HEAD_DIM)
            o = acc[:HEAD_DIM] / acc[HEAD_DIM:HEAD_DIM + 1]
            o_ref[0, rows, :] = (o * gzt_ref[0, rows, :]).astype(o_ref.dtype)


def _dense_attn_kernel(*refs, n_key_blocks, has_sink):
    if has_sink:
        sink_ref, refs = refs[0], refs[1:]
    else:
        sink_ref = None
    qt_ref, k_ref, vt_ref, gzt_ref, o_ref, s_scr, max_scr = refs
    pair = pl.program_id(1)

    def key_range(kb):
        start = kb * KEY_TILE
        if not isinstance(kb, int):
            start = pl.multiple_of(start, KEY_TILE)
        return pl.ds(start, KEY_TILE)

    def k_block(kb):
        return k_ref[0, key_range(kb), :]

    def vt_block(kb):
        return vt_ref[0, :, key_range(kb)]

    per_iter = DENSE_BLOCKS_PER_ITER
    n_iter = (n_key_blocks - 1) // per_iter
    n_tail = n_key_blocks - n_iter * per_iter
    streams = [_HeadsFlash(qt_ref, s_scr, max_scr, pair, range(kv * GROUP, (kv + 1) * GROUP), sink_ref)
               for kv in range(2)]
    slot0 = 0
    streams[0].score(slot0, k_block(0))
    for n, flash in enumerate(streams):

        def run_blocks(first, count, flash=flash, slot0=slot0):
            for u in range(count):
                flash.score((slot0 + u + 1) % 2, k_block(first + u + 1))
                flash.consume((slot0 + u) % 2, vt_block(first + u))

        def body(i, stats, flash=flash, run_blocks=run_blocks):
            flash.stats = list(stats)
            run_blocks(i * per_iter, per_iter)
            return tuple(flash.stats)

        if n_iter == 1:
            run_blocks(0, per_iter)
        elif n_iter > 1:
            flash.stats = list(lax.fori_loop(0, n_iter, body, tuple(flash.stats)))
        run_blocks(n_iter * per_iter, n_tail - 1)
        last_slot = (slot0 + n_key_blocks - 1) % 2
        if n + 1 < len(streams):
            slot0 = 1 - last_slot
            streams[n + 1].score(slot0, k_block(0))
        flash.consume(last_slot, vt_block(n_key_blocks - 1))
        flash.finish(gzt_ref, o_ref)


def _dense_attention(qt, k_all, vt_all, gz, sink):
    b, _, t = qt.shape
    tk = k_all.shape[1]
    tq = Q_TILE
    has_sink = sink is not None
    in_specs = [
        pl.BlockSpec((1, PAIR_WIDTH, tq), lambda bi, j, i: (bi, j, i)),
        pl.BlockSpec((1, tk, LANES), lambda bi, j, i: (bi, 0, j)),
        pl.BlockSpec((1, 2 * HEAD_DIM, tk), lambda bi, j, i: (bi, j, 0)),
        pl.BlockSpec((1, PAIR_WIDTH, tq), lambda bi, j, i: (bi, j, i)),
    ]
    args = [qt, k_all, vt_all, gz]
    if has_sink:
        in_specs = [pl.BlockSpec(memory_space=pltpu.SMEM)] + in_specs
        args = [sink] + args
    return pl.pallas_call(
        functools.partial(_dense_attn_kernel, n_key_blocks=tk // KEY_TILE, has_sink=has_sink),
        grid=(b, N_KV_PAIRS, t // tq),
        in_specs=in_specs,
        out_specs=pl.BlockSpec((1, PAIR_WIDTH, tq), lambda bi, j, i: (bi, j, i)),
        out_shape=jax.ShapeDtypeStruct((b, ATTN_WIDTH, t), BF16),
        scratch_shapes=[pltpu.VMEM((2, GROUP, KEY_TILE, tq), F32),
                        pltpu.VMEM((2, GROUP, 1, tq), F32)],
        compiler_params=pltpu.CompilerParams(vmem_limit_bytes=VMEM_LIMIT),
        name="dense_attn_sink" if has_sink else "dense_attn",
    )(*args)


def _local_attn_kernel(*refs, has_sink, table_heads, n_parts, parts_per_block):
    if has_sink:
        sink_ref, refs = refs[0], refs[1:]
    else:
        sink_ref = None
    qt_ref = refs[0]
    k_parts = refs[1:1 + n_parts]
    kx_ref = refs[1 + n_parts]
    vt_parts = refs[2 + n_parts:2 + 2 * n_parts]
    vx_ref, tab_ref, gzt_ref, o_ref, s_scr, max_scr = refs[2 + 2 * n_parts:]
    pair = pl.program_id(0)

    def bias(n):
        return lambda hh: tab_ref[0, hh if table_heads > 1 else 0, n * KEY_TILE:(n + 1) * KEY_TILE, :]

    def latent_block(n):
        part = range(n * parts_per_block, (n + 1) * parts_per_block)
        return (lambda: jnp.concatenate([k_parts[p][0] for p in part], axis=0),
                lambda: jnp.concatenate([vt_parts[p][0] for p in part], axis=1), bias(n))

    blocks = [(lambda: kx_ref[0], lambda: vx_ref[0], None)]
    blocks += [latent_block(n) for n in range(n_parts // parts_per_block)]
    streams = [_HeadsFlash(qt_ref, s_scr, max_scr, pair, range(kv * GROUP, (kv + 1) * GROUP), sink_ref)
               for kv in range(2)]
    slot0 = 0
    streams[0].score(slot0, blocks[0][0](), blocks[0][2])
    for n, flash in enumerate(streams):
        for u, (_, vt_blk, _) in enumerate(blocks):
            if u + 1 < len(blocks):
                flash.score((slot0 + u + 1) % 2, blocks[u + 1][0](), blocks[u + 1][2])
            elif n + 1 < len(streams):
                slot0 = 1 - (slot0 + u) % 2
                streams[n + 1].score(slot0, blocks[0][0](), blocks[0][2])
                flash.consume(1 - slot0, vt_blk())
                break
            flash.consume((slot0 + u) % 2, vt_blk())
        flash.finish(gzt_ref, o_ref)


def _local_attention(qt, k, vt, k_ctx, vt_ctx, gz, table, sink, *, part, lo_parts):
    b, _, t = qt.shape
    tq = Q_TILE
    nt = t // tq
    has_sink = sink is not None
    table_heads = table.shape[1]
    tab_blk_heads = PAIR_HEADS if table_heads > 1 else 1
    span = table.shape[2]
    n_parts = span // part
    parts_per_block = KEY_TILE // part
    assert span % KEY_TILE == 0 and KEY_TILE % part == 0 and tq % part == 0

    def cls(i):
        return jnp.where(i == 0, 0, jnp.where(i == nt - 1, 2, 1))

    def part_index(i, p):
        return jnp.clip(i * (tq // part) - lo_parts + p, 0, t // part - 1)

    in_specs = [pl.BlockSpec((1, PAIR_WIDTH, tq), lambda j, i, bi: (bi, j, i))]
    in_specs += [pl.BlockSpec((1, part, LANES), lambda j, i, bi, p=p: (bi, part_index(i, p), j))
                 for p in range(n_parts)]
    in_specs += [pl.BlockSpec((1, k_ctx.shape[1], LANES), lambda j, i, bi: (bi, 0, j))]
    in_specs += [pl.BlockSpec((1, 2 * HEAD_DIM, part), lambda j, i, bi, p=p: (bi, j, part_index(i, p)))
                 for p in range(n_parts)]
    in_specs += [
        pl.BlockSpec((1, 2 * HEAD_DIM, vt_ctx.shape[2]), lambda j, i, bi: (bi, j, 0)),
        pl.BlockSpec((1, tab_blk_heads, span, tq),
                     lambda j, i, bi: (cls(i), j if table_heads > 1 else 0, 0, 0)),
        pl.BlockSpec((1, PAIR_WIDTH, tq), lambda j, i, bi: (bi, j, i)),
    ]
    args = [qt] + [k] * n_parts + [k_ctx] + [vt] * n_parts + [vt_ctx, table, gz]
    if has_sink:
        in_specs = [pl.BlockSpec(memory_space=pltpu.SMEM)] + in_specs
        args = [sink] + args
    return pl.pallas_call(
        functools.partial(_local_attn_kernel, has_sink=has_sink, table_heads=table_heads,
                          n_parts=n_parts, parts_per_block=parts_per_block),
        grid=(N_KV_PAIRS, nt, b),
        in_specs=in_specs,
        out_specs=pl.BlockSpec((1, PAIR_WIDTH, tq), lambda j, i, bi: (bi, j, i)),
        out_shape=jax.ShapeDtypeStruct((b, ATTN_WIDTH, t), BF16),
        scratch_shapes=[pltpu.VMEM((2, GROUP, KEY_TILE, tq), F32),
                        pltpu.VMEM((2, GROUP, 1, tq), F32)],
        compiler_params=pltpu.CompilerParams(vmem_limit_bytes=VMEM_LIMIT),
        name="local_attn_sink" if has_sink else "local_attn",
    )(*args)


def _post_kernel(ogt_ref, w_ref, h_ref, mod_ref, gpost_ref, o_ref):
    y = lax.dot_general(ogt_ref[0], w_ref[...], (((0,), (0,)), ((), ())), preferred_element_type=F32)
    ms = jnp.mean(y * y, axis=-1, keepdims=True)
    yn = y * lax.rsqrt(ms + EPS) * gpost_ref[...]
    o_ref[0] = h_ref[0] + mod_ref[0, 2:3, :] * yn


def _post(og, w_out, h, mod, g_post, *, shared_mod):
    b, t, _ = h.shape
    tt = min(POST_TILE, t)
    mod_map = (lambda bi, ti: (0, 0, 0)) if shared_mod else (lambda bi, ti: (bi, 0, 0))
    return pl.pallas_call(
        _post_kernel,
        grid=(b, t // tt),
        in_specs=[
            pl.BlockSpec((1, ATTN_WIDTH, tt), lambda bi, ti: (bi, 0, ti)),
            pl.BlockSpec((ATTN_WIDTH, D_MODEL), lambda bi, ti: (0, 0)),
            pl.BlockSpec((1, tt, D_MODEL), lambda bi, ti: (bi, ti, 0)),
            pl.BlockSpec((1, 3, D_MODEL), mod_map),
            pl.BlockSpec((1, D_MODEL), lambda bi, ti: (0, 0)),
        ],
        out_specs=pl.BlockSpec((1, tt, D_MODEL), lambda bi, ti: (bi, ti, 0)),
        out_shape=jax.ShapeDtypeStruct((b, t, D_MODEL), F32),
        compiler_params=pltpu.CompilerParams(vmem_limit_bytes=VMEM_LIMIT),
        name="post",
    )(og, w_out, h, mod, g_post)


def _rope_tables_t(n):
    t = jnp.arange(n, dtype=jnp.int32)
    row = (t // GRID_W).astype(F32)
    col = (t % GRID_W).astype(F32)
    inv = ROPE_BASE ** (-jnp.arange(ROPE_PAIRS, dtype=F32) / ROPE_PAIRS)
    ang = jnp.concatenate([row[:, None] * inv, col[:, None] * inv], axis=-1)
    return jnp.cos(ang).T, jnp.sin(ang).T


def _tile_positions(n, cls, lo, span):
    nt = n // Q_TILE
    assert nt >= 3 and lo <= Q_TILE and span - lo <= 2 * Q_TILE
    tile = {0: 0, 1: 1, 2: nt - 1}[cls]
    q_pos = tile * Q_TILE + np.arange(Q_TILE)
    k_pos = tile * Q_TILE - lo + np.arange(span)
    in_range = (k_pos >= 0) & (k_pos < n)
    return q_pos, k_pos, in_range


def _window_table(n):
    span = Q_TILE + 2 * WINDOW
    tabs = []
    for cls in range(3):
        q_pos, k_pos, in_range = _tile_positions(n, cls, WINDOW, span)
        valid = (np.abs(k_pos[:, None] - q_pos[None, :]) <= WINDOW) & in_range[:, None]
        assert (valid.sum(axis=0) == np.minimum(q_pos + WINDOW, n - 1) - np.maximum(q_pos - WINDOW, 0) + 1).all()
        tabs.append(np.where(valid, 0.0, NEG_INF).astype(np.float32)[None])
    return jnp.asarray(np.stack(tabs))


def _neighborhood_table(n, bias_table):
    rows = n // GRID_W
    wr = min(NA_MAX_ROWS, rows)
    wc = NA_COLS
    assert Q_TILE % GRID_W == 0
    tile_rows = Q_TILE // GRID_W
    dr_rel = (np.arange(3 * tile_rows)[:, None] - tile_rows - np.arange(tile_rows)[None, :]
              + NA_MAX_ROWS - 1)
    dc_rel = np.arange(GRID_W)[:, None] - np.arange(GRID_W)[None, :] + NA_COLS - 1
    oh_r = (dr_rel[..., None] == np.arange(2 * NA_MAX_ROWS - 1)).astype(np.float32)
    oh_c = (dc_rel[..., None] == np.arange(2 * NA_COLS - 1)).astype(np.float32)
    full = jnp.einsum('kqa,hab,cdb->hkcqd', oh_r, bias_table.astype(F32), oh_c,
                      precision=lax.Precision.HIGHEST)
    full = full.reshape(bias_table.shape[0], 3 * Q_TILE, Q_TILE)
    tabs = []
    for cls in range(3):
        q_pos, k_pos, in_range = _tile_positions(n, cls, Q_TILE, 3 * Q_TILE)
        k_pos = np.clip(k_pos, 0, n - 1)
        r, c_ = q_pos // GRID_W, q_pos % GRID_W
        kr, kc = k_pos // GRID_W, k_pos % GRID_W
        rs = np.clip(r - wr // 2, 0, rows - wr)
        cs = np.clip(c_ - wc // 2, 0, GRID_W - wc)
        valid = ((kr[:, None] >= rs[None, :]) & (kr[:, None] < rs[None, :] + wr)
                 & (kc[:, None] >= cs[None, :]) & (kc[:, None] < cs[None, :] + wc)
                 & in_range[:, None])
        assert (valid.sum(axis=0) == wr * wc).all()
        dr = kr[:, None] - r[None, :] + NA_MAX_ROWS - 1
        dc = kc[:, None] - c_[None, :] + NA_COLS - 1
        dr_tab = np.broadcast_to(dr_rel[:, None, :, None], (3 * tile_rows, GRID_W, tile_rows, GRID_W))
        dc_tab = np.broadcast_to(dc_rel[None, :, None, :], (3 * tile_rows, GRID_W, tile_rows, GRID_W))
        assert (dr_tab.reshape(dr.shape)[valid] == dr[valid]).all()
        assert (dc_tab.reshape(dc.shape)[valid] == dc[valid]).all()
        tabs.append(jnp.where(valid[None], full, NEG_INF))
    return jnp.stack(tabs)


def kernel(x_prompt, x_sample, cache_k, cache_v, c, c_ctx, w_mod, b_mod, norm_pre, norm_post,
           w_in, q_norm, k_norm, w_out, sink_logit, na_rel_bias):
    depth = w_mod.shape[0]
    n_lat = x_sample.shape[1]
    dec_batch = x_sample.shape[0]
    assert dec_batch + 1 <= COND_ROWS
    assert n_lat % Q_TILE == 0 and x_prompt.shape[1] % Q_TILE == 0 and cache_k.shape[2] % KEY_TILE == 0

    cond = jnp.zeros((COND_ROWS, D_MODEL), F32).at[:dec_batch].set(c).at[dec_batch].set(c_ctx)
    mod_all = _modulation(cond, w_mod, b_mod).reshape(depth, COND_ROWS, 3, D_MODEL)

    cos_t, sin_t = _rope_tables_t(n_lat)
    scale = HEAD_DIM ** -0.5
    w_in_b = w_in.astype(BF16)
    w_out_b = w_out.astype(BF16)
    k_ctx_all = cache_k.reshape(dec_batch, depth, -1, KV_WIDTH).astype(BF16)
    vt_ctx_all = cache_v.reshape(dec_batch, depth, -1, KV_WIDTH).transpose(0, 1, 3, 2).astype(BF16)
    win_table = _window_table(n_lat)

    h_ctx, h_lat = x_prompt, x_sample
    new_k, new_v = [], []
    for l in range(depth):
        kind = l % N_MIXERS
        sink = sink_logit[l // N_MIXERS] if kind == 1 else None
        g_pre = norm_pre[l][None]
        g_post = norm_post[l][None]
        qg = jnp.broadcast_to((q_norm[l] * (scale * LOG2E))[:, None], (HEAD_DIM, TOKEN_TILE))
        kg = jnp.broadcast_to(k_norm[l][:, None], (HEAD_DIM, TOKEN_TILE))
        mod_ctx = mod_all[l, dec_batch:dec_batch + 1]
        mod_lat = mod_all[l, :dec_batch]

        qt, k, vt, gz, kf, vf = _pre(h_ctx, mod_ctx, g_pre, w_in_b[l], qg, kg, None, None,
                                     shared_mod=True, emit_kv=True)
        new_k.append(kf.reshape(kf.shape[0], kf.shape[1], N_KV_HEADS, HEAD_DIM))
        new_v.append(vf.reshape(vf.shape[0], vf.shape[1], N_KV_HEADS, HEAD_DIM))
        og = _dense_attention(qt, k, vt, gz, sink)
        h_ctx = _post(og, w_out_b[l], h_ctx, mod_ctx, g_post, shared_mod=True)

        use_rope = kind != 2
        qt, k, vt, gz = _pre(h_lat, mod_lat, g_pre, w_in_b[l], qg, kg,
                             cos_t if use_rope else None, sin_t if use_rope else None,
                             shared_mod=False, emit_kv=False)
        k_ctx, vt_ctx = k_ctx_all[:, l], vt_ctx_all[:, l]
        if kind == 0:
            og = _dense_attention(qt, jnp.concatenate([k, k_ctx], axis=1),
                                  jnp.concatenate([vt, vt_ctx], axis=2), gz, None)
        elif kind == 1:
            og = _local_attention(qt, k, vt, k_ctx, vt_ctx, gz, win_table, sink, part=WINDOW, lo_parts=1)
        else:
            table = _neighborhood_table(n_lat, na_rel_bias[l // N_MIXERS] * LOG2E)
            og = _local_attention(qt, k, vt, k_ctx, vt_ctx, gz, table, None, part=Q_TILE, lo_parts=1)
        h_lat = _post(og, w_out_b[l], h_lat, mod_lat, g_post, shared_mod=False)

    return (h_ctx, h_lat, jnp.stack(new_k, axis=1), jnp.stack(new_v, axis=1))
```

```python
import functools

import jax
import jax.numpy as jnp
import numpy as np
from jax import lax
from jax.experimental import pallas as pl
from jax.experimental.pallas import tpu as pltpu

D_MODEL = 1024
N_HEADS = 16
N_KV_HEADS = 4
HEAD_DIM = 64
GROUP = N_HEADS // N_KV_HEADS
ATTN_WIDTH = N_HEADS * HEAD_DIM
KV_WIDTH = N_KV_HEADS * HEAD_DIM
QK_WIDTH = ATTN_WIDTH + KV_WIDTH
IN_WIDTH = 2 * ATTN_WIDTH + 2 * KV_WIDTH
GRID_W = 64
N_MIXERS = 3
WINDOW = 128
NA_MAX_ROWS = 8
NA_COLS = 16
ROPE_BASE = 10000.0
ROPE_PAIRS = HEAD_DIM // 4
EPS = 1e-6
NEG_INF = -1e30

LANES = 128
N_KV_PAIRS = KV_WIDTH // LANES
PAIR_HEADS = 2 * GROUP
PAIR_WIDTH = PAIR_HEADS * HEAD_DIM
COND_ROWS = 16
Q_TILE = 256
KEY_TILE = 256
SUM_ROWS = 16
DENSE_BLOCKS_PER_ITER = 8
LOG2E = 1.4426950408889634
TOKEN_TILE = 512
POST_TILE = 1024
VMEM_LIMIT = 48 * 1024 * 1024

BF16 = jnp.bfloat16
F32 = jnp.float32


def _silu(x):
    return x * (1.0 / (1.0 + jnp.exp(-x)))


def _mod_kernel(cond_ref, w_ref, b_ref, o_ref):
    a = _silu(cond_ref[...])
    o_ref[0] = jnp.dot(a, w_ref[0], preferred_element_type=F32,
                       precision=lax.Precision.HIGHEST) + b_ref[0]


def _modulation(cond, w_mod, b_mod):
    depth = w_mod.shape[0]
    n_col = 3 * D_MODEL // D_MODEL
    return pl.pallas_call(
        _mod_kernel,
        grid=(depth, n_col),
        in_specs=[
            pl.BlockSpec((COND_ROWS, D_MODEL), lambda l, n: (0, 0)),
            pl.BlockSpec((1, D_MODEL, D_MODEL), lambda l, n: (l, 0, n)),
            pl.BlockSpec((1, 1, D_MODEL), lambda l, n: (l, 0, n)),
        ],
        out_specs=pl.BlockSpec((1, COND_ROWS, D_MODEL), lambda l, n: (l, 0, n)),
        out_shape=jax.ShapeDtypeStruct((depth, COND_ROWS, 3 * D_MODEL), F32),
        compiler_params=pltpu.CompilerParams(vmem_limit_bytes=VMEM_LIMIT),
        name="modulation",
    )(cond, w_mod, b_mod.reshape(depth, 1, 3 * D_MODEL))


def _head_norm_rope_t(xt, gain, cos, sin):
    n_heads = xt.shape[0] // HEAD_DIM
    t = xt.shape[1]
    x3 = xt.reshape(n_heads, HEAD_DIM, t)
    ms = jnp.sum(x3 * x3, axis=1, keepdims=True) * (1.0 / HEAD_DIM)
    y = x3 * lax.rsqrt(ms + EPS) * gain[None]
    if cos is not None:
        half = HEAD_DIM // 2
        y1, y2 = y[:, :half], y[:, half:]
        c_, s_ = cos[None], sin[None]
        y = jnp.concatenate([y1 * c_ - y2 * s_, y2 * c_ + y1 * s_], axis=1)
    return y.reshape(n_heads * HEAD_DIM, t)


def _pre_kernel(*refs, rope, emit_kv):
    h_ref, mod_ref, gpre_ref, w_ref, qg_ref, kg_ref = refs[:6]
    pos = 6
    if rope:
        cos_ref, sin_ref = refs[pos:pos + 2]
        pos += 2
    qt_ref, k_ref, vt_ref, gz_ref = refs[pos:pos + 4]
    pos += 4
    if emit_kv:
        kf_ref, vf_ref = refs[pos:pos + 2]

    x = h_ref[0]
    shift = mod_ref[0, 0:1, :]
    scl = mod_ref[0, 1:2, :]
    ms = jnp.mean(x * x, axis=-1, keepdims=True)
    u = (x * lax.rsqrt(ms + EPS) * gpre_ref[...]) * (1.0 + scl) + shift
    p = jnp.dot(u.astype(BF16), w_ref[...], preferred_element_type=F32)

    z = p[:, QK_WIDTH + KV_WIDTH:]
    gz_ref[0] = _silu(z).T.astype(gz_ref.dtype)
    v = p[:, QK_WIDTH:QK_WIDTH + KV_WIDTH]
    vt_ref[0] = v.T.astype(BF16)

    cos = cos_ref[...] if rope else None
    sin = sin_ref[...] if rope else None
    qt = _head_norm_rope_t(p[:, :ATTN_WIDTH].T, qg_ref[...], cos, sin)
    qt_ref[0] = qt.astype(BF16)
    kt = _head_norm_rope_t(p[:, ATTN_WIDTH:QK_WIDTH].T, kg_ref[...], cos, sin)
    k = kt.T
    k_ref[0] = k.astype(BF16)
    if emit_kv:
        kf_ref[0] = k
        vf_ref[0] = v


def _pre(h, mod, g_pre, w_in, qg, kg, cos_t, sin_t, *, shared_mod, emit_kv):
    b, t, _ = h.shape
    tt = min(TOKEN_TILE, t)
    rope = cos_t is not None
    mod_map = (lambda bi, ti: (0, 0, 0)) if shared_mod else (lambda bi, ti: (bi, 0, 0))
    in_specs = [
        pl.BlockSpec((1, tt, D_MODEL), lambda bi, ti: (bi, ti, 0)),
        pl.BlockSpec((1, 3, D_MODEL), mod_map),
        pl.BlockSpec((1, D_MODEL), lambda bi, ti: (0, 0)),
        pl.BlockSpec((D_MODEL, IN_WIDTH), lambda bi, ti: (0, 0)),
        pl.BlockSpec((HEAD_DIM, tt), lambda bi, ti: (0, 0)),
        pl.BlockSpec((HEAD_DIM, tt), lambda bi, ti: (0, 0)),
    ]
    args = [h, mod, g_pre, w_in, qg, kg]
    if rope:
        in_specs += [pl.BlockSpec((HEAD_DIM // 2, tt), lambda bi, ti: (0, ti))] * 2
        args += [cos_t, sin_t]
    out_specs = [
        pl.BlockSpec((1, ATTN_WIDTH, tt), lambda bi, ti: (bi, 0, ti)),
        pl.BlockSpec((1, tt, KV_WIDTH), lambda bi, ti: (bi, ti, 0)),
        pl.BlockSpec((1, KV_WIDTH, tt), lambda bi, ti: (bi, 0, ti)),
        pl.BlockSpec((1, ATTN_WIDTH, tt), lambda bi, ti: (bi, 0, ti)),
    ]
    out_shape = [
        jax.ShapeDtypeStruct((b, ATTN_WIDTH, t), BF16),
        jax.ShapeDtypeStruct((b, t, KV_WIDTH), BF16),
        jax.ShapeDtypeStruct((b, KV_WIDTH, t), BF16),
        jax.ShapeDtypeStruct((b, ATTN_WIDTH, t), BF16),
    ]
    if emit_kv:
        out_specs += [pl.BlockSpec((1, tt, KV_WIDTH), lambda bi, ti: (bi, ti, 0))] * 2
        out_shape += [jax.ShapeDtypeStruct((b, t, KV_WIDTH), F32)] * 2
    return pl.pallas_call(
        functools.partial(_pre_kernel, rope=rope, emit_kv=emit_kv),
        grid=(b, t // tt),
        in_specs=in_specs,
        out_specs=out_specs,
        out_shape=out_shape,
        compiler_params=pltpu.CompilerParams(vmem_limit_bytes=VMEM_LIMIT),
        name="pre_rope" if rope else "pre_plain",
    )(*args)


def _padded_q(qt, kv_parity):
    zero = jnp.zeros_like(qt)
    if isinstance(kv_parity, int):
        return jnp.concatenate([zero, qt] if kv_parity else [qt, zero], axis=0)
    return jnp.concatenate([jnp.where(kv_parity == 0, qt, zero), jnp.where(kv_parity == 0, zero, qt)], axis=0)


class _HeadsFlash:
    def __init__(self, qt_ref, s_scr, max_scr, pair, heads, sink_ref):
        tq = qt_ref.shape[2]
        self.heads = list(heads)
        self.s_scr = s_scr
        self.max_scr = max_scr
        self.qpads = [_padded_q(qt_ref[0, hh * HEAD_DIM:(hh + 1) * HEAD_DIM, :], hh // GROUP)
                      for hh in self.heads]
        self.ones = jnp.ones((SUM_ROWS, KEY_TILE), BF16)
        self.stats = []
        for hh in self.heads:
            acc = jnp.zeros((HEAD_DIM + SUM_ROWS, tq), F32)
            if sink_ref is None:
                m = jnp.full((1, tq), NEG_INF, F32)
            else:
                m = jnp.full((1, tq), sink_ref[pair * PAIR_HEADS + hh] * LOG2E, F32)
                row = lax.broadcasted_iota(jnp.int32, acc.shape, 0)
                acc = jnp.where(row == HEAD_DIM, 1.0, acc)
            self.stats.append((m, acc))

    def score(self, slot, k_blk, bias=None):
        for n, hh in enumerate(self.heads):
            s = jnp.dot(k_blk, self.qpads[n], preferred_element_type=F32)
            if bias is not None:
                s = s + bias(hh)
            self.s_scr[slot, n] = s
            self.max_scr[slot, n] = jnp.max(s, axis=0, keepdims=True)

    def consume(self, slot, vt_blk):
        vt_aug = {kv: jnp.concatenate([vt_blk[kv * HEAD_DIM:(kv + 1) * HEAD_DIM], self.ones], axis=0)
                  for kv in sorted({hh // GROUP for hh in self.heads})}
        for n, hh in enumerate(self.heads):
            m, acc = self.stats[n]
            m_new = jnp.maximum(m, self.max_scr[slot, n])
            alpha = jnp.exp2(m - m_new)
            p = jnp.exp2((self.s_scr[slot, n] - m_new).astype(BF16))
            acc = alpha * acc + jnp.dot(vt_aug[hh // GROUP], p, preferred_element_type=F32)
            self.stats[n] = (m_new, acc)

    def finish(self, gzt_ref, o_ref):
        for n, hh in enumerate(self.heads):
            _, acc = self.stats[n]
            rows = slice(hh * HEAD_DIM, (hh + 1) * HEAD_DIM)
            o = acc[:HEAD_DIM] / acc[HEAD_DIM:HEAD_DIM + 1]
            o_ref[0, rows, :] = (o * gzt_ref[0, rows, :]).astype(o_ref.dtype)


def _dense_attn_kernel(*refs, n_key_blocks, has_sink):
    if has_sink:
        sink_ref, refs = refs[0], refs[1:]
    else:
        sink_ref = None
    qt_ref, k_ref, vt_ref, gzt_ref, o_ref, s_scr, max_scr = refs
    pair = pl.program_id(1)

    def key_range(kb):
        start = kb * KEY_TILE
        if not isinstance(kb, int):
            start = pl.multiple_of(start, KEY_TILE)
        return pl.ds(start, KEY_TILE)

    def k_block(kb):
        return k_ref[0, key_range(kb), :]

    def vt_block(kb):
        return vt_ref[0, :, key_range(kb)]

    per_iter = DENSE_BLOCKS_PER_ITER
    n_iter = (n_key_blocks - 1) // per_iter
    n_tail = n_key_blocks - n_iter * per_iter
    streams = [_HeadsFlash(qt_ref, s_scr, max_scr, pair, range(kv * GROUP, (kv + 1) * GROUP), sink_ref)
               for kv in range(2)]
    slot0 = 0
    streams[0].score(slot0, k_block(0))
    for n, flash in enumerate(streams):

        def run_blocks(first, count, flash=flash, slot0=slot0):
            for u in range(count):
                flash.score((slot0 + u + 1) % 2, k_block(first + u + 1))
                flash.consume((slot0 + u) % 2, vt_block(first + u))

        def body(i, stats, flash=flash, run_blocks=run_blocks):
            flash.stats = list(stats)
            run_blocks(i * per_iter, per_iter)
            return tuple(flash.stats)

        if n_iter == 1:
            run_blocks(0, per_iter)
        elif n_iter > 1:
            flash.stats = list(lax.fori_loop(0, n_iter, body, tuple(flash.stats)))
        run_blocks(n_iter * per_iter, n_tail - 1)
        last_slot = (slot0 + n_key_blocks - 1) % 2
        if n + 1 < len(streams):
            slot0 = 1 - last_slot
            streams[n + 1].score(slot0, k_block(0))
        flash.consume(last_slot, vt_block(n_key_blocks - 1))
        flash.finish(gzt_ref, o_ref)


def _dense_attention(qt, k_all, vt_all, gz, sink):
    b, _, t = qt.shape
    tk = k_all.shape[1]
    tq = Q_TILE
    has_sink = sink is not None
    in_specs = [
        pl.BlockSpec((1, PAIR_WIDTH, tq), lambda bi, j, i: (bi, j, i)),
        pl.BlockSpec((1, tk, LANES), lambda bi, j, i: (bi, 0, j)),
        pl.BlockSpec((1, 2 * HEAD_DIM, tk), lambda bi, j, i: (bi, j, 0)),
        pl.BlockSpec((1, PAIR_WIDTH, tq), lambda bi, j, i: (bi, j, i)),
    ]
    args = [qt, k_all, vt_all, gz]
    if has_sink:
        in_specs = [pl.BlockSpec(memory_space=pltpu.SMEM)] + in_specs
        args = [sink] + args
    return pl.pallas_call(
        functools.partial(_dense_attn_kernel, n_key_blocks=tk // KEY_TILE, has_sink=has_sink),
        grid=(b, N_KV_PAIRS, t // tq),
        in_specs=in_specs,
        out_specs=pl.BlockSpec((1, PAIR_WIDTH, tq), lambda bi, j, i: (bi, j, i)),
        out_shape=jax.ShapeDtypeStruct((b, ATTN_WIDTH, t), BF16),
        scratch_shapes=[pltpu.VMEM((2, GROUP, KEY_TILE, tq), F32),
                        pltpu.VMEM((2, GROUP, 1, tq), F32)],
        compiler_params=pltpu.CompilerParams(vmem_limit_bytes=VMEM_LIMIT),
        name="dense_attn_sink" if has_sink else "dense_attn",
    )(*args)


def _local_attn_kernel(*refs, has_sink, table_heads, n_parts, parts_per_block):
    if has_sink:
        sink_ref, refs = refs[0], refs[1:]
    else:
        sink_ref = None
    qt_ref = refs[0]
    k_parts = refs[1:1 + n_parts]
    kx_ref = refs[1 + n_parts]
    vt_parts = refs[2 + n_parts:2 + 2 * n_parts]
    vx_ref, tab_ref, gzt_ref, o_ref, s_scr, max_scr = refs[2 + 2 * n_parts:]
    pair = pl.program_id(0)

    def bias(n):
        return lambda hh: tab_ref[0, hh if table_heads > 1 else 0, n * KEY_TILE:(n + 1) * KEY_TILE, :]

    def latent_block(n):
        part = range(n * parts_per_block, (n + 1) * parts_per_block)
        return (lambda: jnp.concatenate([k_parts[p][0] for p in part], axis=0),
                lambda: jnp.concatenate([vt_parts[p][0] for p in part], axis=1), bias(n))

    blocks = [(lambda: kx_ref[0], lambda: vx_ref[0], None)]
    blocks += [latent_block(n) for n in range(n_parts // parts_per_block)]
    streams = [_HeadsFlash(qt_ref, s_scr, max_scr, pair, range(kv * GROUP, (kv + 1) * GROUP), sink_ref)
               for kv in range(2)]
    slot0 = 0
    streams[0].score(slot0, blocks[0][0](), blocks[0][2])
    for n, flash in enumerate(streams):
        for u, (_, vt_blk, _) in enumerate(blocks):
            if u + 1 < len(blocks):
                flash.score((slot0 + u + 1) % 2, blocks[u + 1][0](), blocks[u + 1][2])
            elif n + 1 < len(streams):
                slot0 = 1 - (slot0 + u) % 2
                streams[n + 1].score(slot0, blocks[0][0](), blocks[0][2])
                flash.consume(1 - slot0, vt_blk())
                break
            flash.consume((slot0 + u) % 2, vt_blk())
        flash.finish(gzt_ref, o_ref)


def _local_attention(qt, k, vt, k_ctx, vt_ctx, gz, table, sink, *, part, lo_parts):
    b, _, t = qt.shape
    tq = Q_TILE
    nt = t // tq
    has_sink = sink is not None
    table_heads = table.shape[1]
    tab_blk_heads = PAIR_HEADS if table_heads > 1 else 1
    span = table.shape[2]
    n_parts = span // part
    parts_per_block = KEY_TILE // part
    assert span % KEY_TILE == 0 and KEY_TILE % part == 0 and tq % part == 0

    def cls(i):
        return jnp.where(i == 0, 0, jnp.where(i == nt - 1, 2, 1))

    def part_index(i, p):
        return jnp.clip(i * (tq // part) - lo_parts + p, 0, t // part - 1)

    in_specs = [pl.BlockSpec((1, PAIR_WIDTH, tq), lambda j, i, bi: (bi, j, i))]
    in_specs += [pl.BlockSpec((1, part, LANES), lambda j, i, bi, p=p: (bi, part_index(i, p), j))
                 for p in range(n_parts)]
    in_specs += [pl.BlockSpec((1, k_ctx.shape[1], LANES), lambda j, i, bi: (bi, 0, j))]
    in_specs += [pl.BlockSpec((1, 2 * HEAD_DIM, part), lambda j, i, bi, p=p: (bi, j, part_index(i, p)))
                 for p in range(n_parts)]
    in_specs += [
        pl.BlockSpec((1, 2 * HEAD_DIM, vt_ctx.shape[2]), lambda j, i, bi: (bi, j, 0)),
        pl.BlockSpec((1, tab_blk_heads, span, tq),
                     lambda j, i, bi: (cls(i), j if table_heads > 1 else 0, 0, 0)),
        pl.BlockSpec((1, PAIR_WIDTH, tq), lambda j, i, bi: (bi, j, i)),
    ]
    args = [qt] + [k] * n_parts + [k_ctx] + [vt] * n_parts + [vt_ctx, table, gz]
    if has_sink:
        in_specs = [pl.BlockSpec(memory_space=pltpu.SMEM)] + in_specs
        args = [sink] + args
    return pl.pallas_call(
        functools.partial(_local_attn_kernel, has_sink=has_sink, table_heads=table_heads,
                          n_parts=n_parts, parts_per_block=parts_per_block),
        grid=(N_KV_PAIRS, nt, b),
        in_specs=in_specs,
        out_specs=pl.BlockSpec((1, PAIR_WIDTH, tq), lambda j, i, bi: (bi, j, i)),
        out_shape=jax.ShapeDtypeStruct((b, ATTN_WIDTH, t), BF16),
        scratch_shapes=[pltpu.VMEM((2, GROUP, KEY_TILE, tq), F32),
                        pltpu.VMEM((2, GROUP, 1, tq), F32)],
        compiler_params=pltpu.CompilerParams(vmem_limit_bytes=VMEM_LIMIT),
        name="local_attn_sink" if has_sink else "local_attn",
    )(*args)


def _post_kernel(ogt_ref, w_ref, h_ref, mod_ref, gpost_ref, o_ref):
    y = lax.dot_general(ogt_ref[0], w_ref[...], (((0,), (0,)), ((), ())), preferred_element_type=F32)
    ms = jnp.mean(y * y, axis=-1, keepdims=True)
    yn = y * lax.rsqrt(ms + EPS) * gpost_ref[...]
    o_ref[0] = h_ref[0] + mod_ref[0, 2:3, :] * yn


def _post(og, w_out, h, mod, g_post, *, shared_mod):
    b, t, _ = h.shape
    tt = min(POST_TILE, t)
    mod_map = (lambda bi, ti: (0, 0, 0)) if shared_mod else (lambda bi, ti: (bi, 0, 0))
    return pl.pallas_call(
        _post_kernel,
        grid=(b, t // tt),
        in_specs=[
            pl.BlockSpec((1, ATTN_WIDTH, tt), lambda bi, ti: (bi, 0, ti)),
            pl.BlockSpec((ATTN_WIDTH, D_MODEL), lambda bi, ti: (0, 0)),
            pl.BlockSpec((1, tt, D_MODEL), lambda bi, ti: (bi, ti, 0)),
            pl.BlockSpec((1, 3, D_MODEL), mod_map),
            pl.BlockSpec((1, D_MODEL), lambda bi, ti: (0, 0)),
        ],
        out_specs=pl.BlockSpec((1, tt, D_MODEL), lambda bi, ti: (bi, ti, 0)),
        out_shape=jax.ShapeDtypeStruct((b, t, D_MODEL), F32),
        compiler_params=pltpu.CompilerParams(vmem_limit_bytes=VMEM_LIMIT),
        name="post",
    )(og, w_out, h, mod, g_post)


def _rope_tables_t(n):
    t = jnp.arange(n, dtype=jnp.int32)
    row = (t // GRID_W).astype(F32)
    col = (t % GRID_W).astype(F32)
    inv = ROPE_BASE ** (-jnp.arange(ROPE_PAIRS, dtype=F32) / ROPE_PAIRS)
    ang = jnp.concatenate([row[:, None] * inv, col[:, None] * inv], axis=-1)
    return jnp.cos(ang).T, jnp.sin(ang).T


def _tile_positions(n, cls, lo, span):
    nt = n // Q_TILE
    assert nt >= 3 and lo <= Q_TILE and span - lo <= 2 * Q_TILE
    tile = {0: 0, 1: 1, 2: nt - 1}[cls]
    q_pos = tile * Q_TILE + np.arange(Q_TILE)
    k_pos = tile * Q_TILE - lo + np.arange(span)
    in_range = (k_pos >= 0) & (k_pos < n)
    return q_pos, k_pos, in_range


def _window_table(n):
    span = Q_TILE + 2 * WINDOW
    tabs = []
    for cls in range(3):
        q_pos, k_pos, in_range = _tile_positions(n, cls, WINDOW, span)
        valid = (np.abs(k_pos[:, None] - q_pos[None, :]) <= WINDOW) & in_range[:, None]
        assert (valid.sum(axis=0) == np.minimum(q_pos + WINDOW, n - 1) - np.maximum(q_pos - WINDOW, 0) + 1).all()
        tabs.append(np.where(valid, 0.0, NEG_INF).astype(np.float32)[None])
    return jnp.asarray(np.stack(tabs))


def _neighborhood_table(n, bias_table):
    rows = n // GRID_W
    wr = min(NA_MAX_ROWS, rows)
    wc = NA_COLS
    assert Q_TILE % GRID_W == 0
    tile_rows = Q_TILE // GRID_W
    dr_rel = (np.arange(3 * tile_rows)[:, None] - tile_rows - np.arange(tile_rows)[None, :]
              + NA_MAX_ROWS - 1)
    dc_rel = np.arange(GRID_W)[:, None] - np.arange(GRID_W)[None, :] + NA_COLS - 1
    oh_r = (dr_rel[..., None] == np.arange(2 * NA_MAX_ROWS - 1)).astype(np.float32)
    oh_c = (dc_rel[..., None] == np.arange(2 * NA_COLS - 1)).astype(np.float32)
    full = jnp.einsum('kqa,hab,cdb->hkcqd', oh_r, bias_table.astype(F32), oh_c,
                      precision=lax.Precision.HIGHEST)
    full = full.reshape(bias_table.shape[0], 3 * Q_TILE, Q_TILE)
    tabs = []
    for cls in range(3):
        q_pos, k_pos, in_range = _tile_positions(n, cls, Q_TILE, 3 * Q_TILE)
        k_pos = np.clip(k_pos, 0, n - 1)
        r, c_ = q_pos // GRID_W, q_pos % GRID_W
        kr, kc = k_pos // GRID_W, k_pos % GRID_W
        rs = np.clip(r - wr // 2, 0, rows - wr)
        cs = np.clip(c_ - wc // 2, 0, GRID_W - wc)
        valid = ((kr[:, None] >= rs[None, :]) & (kr[:, None] < rs[None, :] + wr)
                 & (kc[:, None] >= cs[None, :]) & (kc[:, None] < cs[None, :] + wc)
                 & in_range[:, None])
        assert (valid.sum(axis=0) == wr * wc).all()
        dr = kr[:, None] - r[None, :] + NA_MAX_ROWS - 1
        dc = kc[:, None] - c_[None, :] + NA_COLS - 1
        dr_tab = np.broadcast_to(dr_rel[:, None, :, None], (3 * tile_rows, GRID_W, tile_rows, GRID_W))
        dc_tab = np.broadcast_to(dc_rel[None, :, None, :], (3 * tile_rows, GRID_W, tile_rows, GRID_W))
        assert (dr_tab.reshape(dr.shape)[valid] == dr[valid]).all()
        assert (dc_tab.reshape(dc.shape)[valid] == dc[valid]).all()
        tabs.append(jnp.where(valid[None], full, NEG_INF))
    return jnp.stack(tabs)


def kernel(x_prompt, x_sample, cache_k, cache_v, c, c_ctx, w_mod, b_mod, norm_pre, norm_post,
           w_in, q_norm, k_norm, w_out, sink_logit, na_rel_bias):
    depth = w_mod.shape[0]
    n_lat = x_sample.shape[1]
    dec_batch = x_sample.shape[0]
    assert dec_batch + 1 <= COND_ROWS
    assert n_lat % Q_TILE == 0 and x_prompt.shape[1] % Q_TILE == 0 and cache_k.shape[2] % KEY_TILE == 0

    cond = jnp.zeros((COND_ROWS, D_MODEL), F32).at[:dec_batch].set(c).at[dec_batch].set(c_ctx)
    mod_all = _modulation(cond, w_mod, b_mod).reshape(depth, COND_ROWS, 3, D_MODEL)

    cos_t, sin_t = _rope_tables_t(n_lat)
    scale = HEAD_DIM ** -0.5
    w_in_b = w_in.astype(BF16)
    w_out_b = w_out.astype(BF16)
    k_ctx_all = cache_k.reshape(dec_batch, depth, -1, KV_WIDTH).astype(BF16)
    vt_ctx_all = cache_v.reshape(dec_batch, depth, -1, KV_WIDTH).transpose(0, 1, 3, 2).astype(BF16)
    win_table = _window_table(n_lat)

    h_ctx, h_lat = x_prompt, x_sample
    new_k, new_v = [], []
    for l in range(depth):
        kind = l % N_MIXERS
        sink = sink_logit[l // N_MIXERS] if kind == 1 else None
        g_pre = norm_pre[l][None]
        g_post = norm_post[l][None]
        qg = jnp.broadcast_to((q_norm[l] * (scale * LOG2E))[:, None], (HEAD_DIM, TOKEN_TILE))
        kg = jnp.broadcast_to(k_norm[l][:, None], (HEAD_DIM, TOKEN_TILE))
        mod_ctx = mod_all[l, dec_batch:dec_batch + 1]
        mod_lat = mod_all[l, :dec_batch]

        qt, k, vt, gz, kf, vf = _pre(h_ctx, mod_ctx, g_pre, w_in_b[l], qg, kg, None, None,
                                     shared_mod=True, emit_kv=True)
        new_k.append(kf.reshape(kf.shape[0], kf.shape[1], N_KV_HEADS, HEAD_DIM))
        new_v.append(vf.reshape(vf.shape[0], vf.shape[1], N_KV_HEADS, HEAD_DIM))
        og = _dense_attention(qt, k, vt, gz, sink)
        h_ctx = _post(og, w_out_b[l], h_ctx, mod_ctx, g_post, shared_mod=True)

        use_rope = kind != 2
        qt, k, vt, gz = _pre(h_lat, mod_lat, g_pre, w_in_b[l], qg, kg,
                             cos_t if use_rope else None, sin_t if use_rope else None,
                             shared_mod=False, emit_kv=False)
        k_ctx, vt_ctx = k_ctx_all[:, l], vt_ctx_all[:, l]
        if kind == 0:
            og = _dense_attention(qt, jnp.concatenate([k, k_ctx], axis=1),
                                  jnp.concatenate([vt, vt_ctx], axis=2), gz, None)
        elif kind == 1:
            og = _local_attention(qt, k, vt, k_ctx, vt_ctx, gz, win_table, sink, part=WINDOW, lo_parts=1)
        else:
            table = _neighborhood_table(n_lat, na_rel_bias[l // N_MIXERS] * LOG2E)
            og = _local_attention(qt, k, vt, k_ctx, vt_ctx, gz, table, None, part=Q_TILE, lo_parts=1)
        h_lat = _post(og, w_out_b[l], h_lat, mod_lat, g_post, shared_mod=False)

    return (h_ctx, h_lat, jnp.stack(new_k, axis=1), jnp.stack(new_v, axis=1))
```

```python
import functools

import jax
import jax.numpy as jnp
import numpy as np
from jax import lax
from jax.experimental import pallas as pl
from jax.experimental.pallas import tpu as pltpu

D_MODEL = 1024
N_HEADS = 16
N_KV_HEADS = 4
HEAD_DIM = 64
GROUP = N_HEADS // N_KV_HEADS
ATTN_WIDTH = N_HEADS * HEAD_DIM
KV_WIDTH = N_KV_HEADS * HEAD_DIM
QK_WIDTH = ATTN_WIDTH + KV_WIDTH
IN_WIDTH = 2 * ATTN_WIDTH + 2 * KV_WIDTH
GRID_W = 64
N_MIXERS = 3
WINDOW = 128
NA_MAX_ROWS = 8
NA_COLS = 16
ROPE_BASE = 10000.0
ROPE_PAIRS = HEAD_DIM // 4
EPS = 1e-6
NEG_INF = -1e30

LANES = 128
N_KV_PAIRS = KV_WIDTH // LANES
PAIR_HEADS = 2 * GROUP
PAIR_WIDTH = PAIR_HEADS * HEAD_DIM
COND_ROWS = 16
Q_TILE = 256
KEY_TILE = 256
SUM_ROWS = 16
DENSE_BLOCKS_PER_ITER = 8
DENSE_PAIRS_PER_STEP = 1
LOCAL_PAIRS_PER_STEP = 2
LOG2E = 1.4426950408889634
TOKEN_TILE = 512
POST_TILE = 1024
VMEM_LIMIT = 48 * 1024 * 1024

BF16 = jnp.bfloat16
F32 = jnp.float32


def _silu(x):
    return x * (1.0 / (1.0 + jnp.exp(-x)))


def _mod_kernel(cond_ref, w_ref, b_ref, o_ref):
    a = _silu(cond_ref[...])
    o_ref[0] = jnp.dot(a, w_ref[0], preferred_element_type=F32,
                       precision=lax.Precision.HIGHEST) + b_ref[0]


def _modulation(cond, w_mod, b_mod):
    depth = w_mod.shape[0]
    n_col = 3 * D_MODEL // D_MODEL
    return pl.pallas_call(
        _mod_kernel,
        grid=(depth, n_col),
        in_specs=[
            pl.BlockSpec((COND_ROWS, D_MODEL), lambda l, n: (0, 0)),
            pl.BlockSpec((1, D_MODEL, D_MODEL), lambda l, n: (l, 0, n)),
            pl.BlockSpec((1, 1, D_MODEL), lambda l, n: (l, 0, n)),
        ],
        out_specs=pl.BlockSpec((1, COND_ROWS, D_MODEL), lambda l, n: (l, 0, n)),
        out_shape=jax.ShapeDtypeStruct((depth, COND_ROWS, 3 * D_MODEL), F32),
        compiler_params=pltpu.CompilerParams(vmem_limit_bytes=VMEM_LIMIT),
        name="modulation",
    )(cond, w_mod, b_mod.reshape(depth, 1, 3 * D_MODEL))


def _head_norm_rope_t(xt, gain, cos, sin):
    n_heads = xt.shape[0] // HEAD_DIM
    t = xt.shape[1]
    x3 = xt.reshape(n_heads, HEAD_DIM, t)
    ms = jnp.sum(x3 * x3, axis=1, keepdims=True) * (1.0 / HEAD_DIM)
    y = x3 * lax.rsqrt(ms + EPS) * gain[None]
    if cos is not None:
        half = HEAD_DIM // 2
        y1, y2 = y[:, :half], y[:, half:]
        c_, s_ = cos[None], sin[None]
        y = jnp.concatenate([y1 * c_ - y2 * s_, y2 * c_ + y1 * s_], axis=1)
    return y.reshape(n_heads * HEAD_DIM, t)


def _pre_kernel(*refs, rope, emit_kv):
    h_ref, mod_ref, gpre_ref, w_ref, qg_ref, kg_ref = refs[:6]
    pos = 6
    if rope:
        cos_ref, sin_ref = refs[pos:pos + 2]
        pos += 2
    qt_ref, k_ref, vt_ref, gz_ref = refs[pos:pos + 4]
    pos += 4
    if emit_kv:
        kf_ref, vf_ref = refs[pos:pos + 2]

    x = h_ref[0]
    shift = mod_ref[0, 0:1, :]
    scl = mod_ref[0, 1:2, :]
    ms = jnp.mean(x * x, axis=-1, keepdims=True)
    u = (x * lax.rsqrt(ms + EPS) * gpre_ref[...]) * (1.0 + scl) + shift
    p = jnp.dot(u.astype(BF16), w_ref[...], preferred_element_type=F32)

    z = p[:, QK_WIDTH + KV_WIDTH:]
    gz_ref[0] = _silu(z).T.astype(gz_ref.dtype)
    v = p[:, QK_WIDTH:QK_WIDTH + KV_WIDTH]
    vt_ref[0] = v.T.astype(BF16)

    cos = cos_ref[...] if rope else None
    sin = sin_ref[...] if rope else None
    qt = _head_norm_rope_t(p[:, :ATTN_WIDTH].T, qg_ref[...], cos, sin)
    qt_ref[0] = qt.astype(BF16)
    kt = _head_norm_rope_t(p[:, ATTN_WIDTH:QK_WIDTH].T, kg_ref[...], cos, sin)
    k = kt.T
    k_ref[0] = k.astype(BF16)
    if emit_kv:
        kf_ref[0] = k
        vf_ref[0] = v


def _pre(h, mod, g_pre, w_in, qg, kg, cos_t, sin_t, *, shared_mod, emit_kv):
    b, t, _ = h.shape
    tt = min(TOKEN_TILE, t)
    rope = cos_t is not None
    mod_map = (lambda bi, ti: (0, 0, 0)) if shared_mod else (lambda bi, ti: (bi, 0, 0))
    in_specs = [
        pl.BlockSpec((1, tt, D_MODEL), lambda bi, ti: (bi, ti, 0)),
        pl.BlockSpec((1, 3, D_MODEL), mod_map),
        pl.BlockSpec((1, D_MODEL), lambda bi, ti: (0, 0)),
        pl.BlockSpec((D_MODEL, IN_WIDTH), lambda bi, ti: (0, 0)),
        pl.BlockSpec((HEAD_DIM, tt), lambda bi, ti: (0, 0)),
        pl.BlockSpec((HEAD_DIM, tt), lambda bi, ti: (0, 0)),
    ]
    args = [h, mod, g_pre, w_in, qg, kg]
    if rope:
        in_specs += [pl.BlockSpec((HEAD_DIM // 2, tt), lambda bi, ti: (0, ti))] * 2
        args += [cos_t, sin_t]
    out_specs = [
        pl.BlockSpec((1, ATTN_WIDTH, tt), lambda bi, ti: (bi, 0, ti)),
        pl.BlockSpec((1, tt, KV_WIDTH), lambda bi, ti: (bi, ti, 0)),
        pl.BlockSpec((1, KV_WIDTH, tt), lambda bi, ti: (bi, 0, ti)),
        pl.BlockSpec((1, ATTN_WIDTH, tt), lambda bi, ti: (bi, 0, ti)),
    ]
    out_shape = [
        jax.ShapeDtypeStruct((b, ATTN_WIDTH, t), BF16),
        jax.ShapeDtypeStruct((b, t, KV_WIDTH), BF16),
        jax.ShapeDtypeStruct((b, KV_WIDTH, t), BF16),
        jax.ShapeDtypeStruct((b, ATTN_WIDTH, t), BF16),
    ]
    if emit_kv:
        out_specs += [pl.BlockSpec((1, tt, KV_WIDTH), lambda bi, ti: (bi, ti, 0))] * 2
        out_shape += [jax.ShapeDtypeStruct((b, t, KV_WIDTH), F32)] * 2
    return pl.pallas_call(
        functools.partial(_pre_kernel, rope=rope, emit_kv=emit_kv),
        grid=(b, t // tt),
        in_specs=in_specs,
        out_specs=out_specs,
        out_shape=out_shape,
        compiler_params=pltpu.CompilerParams(vmem_limit_bytes=VMEM_LIMIT),
        name="pre_rope" if rope else "pre_plain",
    )(*args)


def _padded_q(qt, kv_parity):
    zero = jnp.zeros_like(qt)
    if isinstance(kv_parity, int):
        return jnp.concatenate([zero, qt] if kv_parity else [qt, zero], axis=0)
    return jnp.concatenate([jnp.where(kv_parity == 0, qt, zero), jnp.where(kv_parity == 0, zero, qt)], axis=0)


class _HeadsFlash:
    def __init__(self, qt_ref, s_scr, max_scr, head0, heads, sink_ref):
        tq = qt_ref.shape[2]
        self.heads = list(heads)
        self.s_scr = s_scr
        self.max_scr = max_scr
        self.qpads = [_padded_q(qt_ref[0, hh * HEAD_DIM:(hh + 1) * HEAD_DIM, :], (hh // GROUP) % 2)
                      for hh in self.heads]
        self.ones = jnp.ones((SUM_ROWS, KEY_TILE), BF16)
        self.stats = []
        for hh in self.heads:
            acc = jnp.zeros((HEAD_DIM + SUM_ROWS, tq), F32)
            if sink_ref is None:
                m = jnp.full((1, tq), NEG_INF, F32)
            else:
                m = jnp.full((1, tq), sink_ref[head0 + hh] * LOG2E, F32)
                row = lax.broadcasted_iota(jnp.int32, acc.shape, 0)
                acc = jnp.where(row == HEAD_DIM, 1.0, acc)
            self.stats.append((m, acc))

    def score(self, slot, k_blk, bias=None):
        for n, hh in enumerate(self.heads):
            pair = hh // PAIR_HEADS
            s = jnp.dot(k_blk[:, pair * LANES:(pair + 1) * LANES], self.qpads[n], preferred_element_type=F32)
            if bias is not None:
                s = s + bias(hh)
            self.s_scr[slot, n] = s
            self.max_scr[slot, n] = jnp.max(s, axis=0, keepdims=True)

    def consume(self, slot, vt_blk):
        vt_aug = {kv: jnp.concatenate([vt_blk[kv * HEAD_DIM:(kv + 1) * HEAD_DIM], self.ones], axis=0)
                  for kv in sorted({hh // GROUP for hh in self.heads})}
        for n, hh in enumerate(self.heads):
            m, acc = self.stats[n]
            m_new = jnp.maximum(m, self.max_scr[slot, n])
            alpha = jnp.exp2(m - m_new)
            p = jnp.exp2((self.s_scr[slot, n] - m_new).astype(BF16))
            acc = alpha * acc + jnp.dot(vt_aug[hh // GROUP], p, preferred_element_type=F32)
            self.stats[n] = (m_new, acc)

    def finish(self, gzt_ref, o_ref):
        for n, hh in enumerate(self.heads):
            _, acc = self.stats[n]
            rows = slice(hh * HEAD_DIM, (hh + 1) * HEAD_DIM)
            o = acc[:HEAD_DIM] / acc[HEAD_DIM:HEAD_DIM + 1]
            o_ref[0, rows, :] = (o * gzt_ref[0, rows, :]).astype(o_ref.dtype)


def _dense_attn_kernel(*refs, n_key_blocks, has_sink, pairs):
    if has_sink:
        sink_ref, refs = refs[0], refs[1:]
    else:
        sink_ref = None
    qt_ref, k_ref, vt_ref, gzt_ref, o_ref, s_scr, max_scr = refs
    head0 = pl.program_id(1) * (pairs * PAIR_HEADS)

    def key_range(kb):
        start = kb * KEY_TILE
        if not isinstance(kb, int):
            start = pl.multiple_of(start, KEY_TILE)
        return pl.ds(start, KEY_TILE)

    def k_block(kb):
        return k_ref[0, key_range(kb), :]

    def vt_block(kb):
        return vt_ref[0, :, key_range(kb)]

    per_iter = DENSE_BLOCKS_PER_ITER
    n_iter = (n_key_blocks - 1) // per_iter
    n_tail = n_key_blocks - n_iter * per_iter
    streams = [_HeadsFlash(qt_ref, s_scr, max_scr, head0, range(kv * GROUP, (kv + 1) * GROUP), sink_ref)
               for kv in range(2 * pairs)]
    slot0 = 0
    streams[0].score(slot0, k_block(0))
    for n, flash in enumerate(streams):

        def run_blocks(first, count, flash=flash, slot0=slot0):
            for u in range(count):
                flash.score((slot0 + u + 1) % 2, k_block(first + u + 1))
                flash.consume((slot0 + u) % 2, vt_block(first + u))

        def body(i, stats, flash=flash, run_blocks=run_blocks):
            flash.stats = list(stats)
            run_blocks(i * per_iter, per_iter)
            return tuple(flash.stats)

        if n_iter == 1:
            run_blocks(0, per_iter)
        elif n_iter > 1:
            flash.stats = list(lax.fori_loop(0, n_iter, body, tuple(flash.stats)))
        run_blocks(n_iter * per_iter, n_tail - 1)
        last_slot = (slot0 + n_key_blocks - 1) % 2
        if n + 1 < len(streams):
            slot0 = 1 - last_slot
            streams[n + 1].score(slot0, k_block(0))
        flash.consume(last_slot, vt_block(n_key_blocks - 1))
        flash.finish(gzt_ref, o_ref)


def _dense_attention(qt, k_all, vt_all, gz, sink):
    b, _, t = qt.shape
    tk = k_all.shape[1]
    tq = Q_TILE
    has_sink = sink is not None
    pairs = DENSE_PAIRS_PER_STEP
    in_specs = [
        pl.BlockSpec((1, pairs * PAIR_WIDTH, tq), lambda bi, j, i: (bi, j, i)),
        pl.BlockSpec((1, tk, pairs * LANES), lambda bi, j, i: (bi, 0, j)),
        pl.BlockSpec((1, pairs * 2 * HEAD_DIM, tk), lambda bi, j, i: (bi, j, 0)),
        pl.BlockSpec((1, pairs * PAIR_WIDTH, tq), lambda bi, j, i: (bi, j, i)),
    ]
    args = [qt, k_all, vt_all, gz]
    if has_sink:
        in_specs = [pl.BlockSpec(memory_space=pltpu.SMEM)] + in_specs
        args = [sink] + args
    return pl.pallas_call(
        functools.partial(_dense_attn_kernel, n_key_blocks=tk // KEY_TILE, has_sink=has_sink, pairs=pairs),
        grid=(b, N_KV_PAIRS // pairs, t // tq),
        in_specs=in_specs,
        out_specs=pl.BlockSpec((1, pairs * PAIR_WIDTH, tq), lambda bi, j, i: (bi, j, i)),
        out_shape=jax.ShapeDtypeStruct((b, ATTN_WIDTH, t), BF16),
        scratch_shapes=[pltpu.VMEM((2, GROUP, KEY_TILE, tq), F32),
                        pltpu.VMEM((2, GROUP, 1, tq), F32)],
        compiler_params=pltpu.CompilerParams(vmem_limit_bytes=VMEM_LIMIT),
        name="dense_attn_sink" if has_sink else "dense_attn",
    )(*args)


def _local_attn_kernel(*refs, has_sink, table_heads, n_parts, parts_per_block, pairs):
    if has_sink:
        sink_ref, refs = refs[0], refs[1:]
    else:
        sink_ref = None
    qt_ref = refs[0]
    k_parts = refs[1:1 + n_parts]
    kx_ref = refs[1 + n_parts]
    vt_parts = refs[2 + n_parts:2 + 2 * n_parts]
    vx_ref, tab_ref, gzt_ref, o_ref, s_scr, max_scr = refs[2 + 2 * n_parts:]
    head0 = pl.program_id(0) * (pairs * PAIR_HEADS)

    def bias(n):
        return lambda hh: tab_ref[0, hh if table_heads > 1 else 0, n * KEY_TILE:(n + 1) * KEY_TILE, :]

    def latent_block(n):
        part = range(n * parts_per_block, (n + 1) * parts_per_block)
        return (lambda: jnp.concatenate([k_parts[p][0] for p in part], axis=0),
                lambda: jnp.concatenate([vt_parts[p][0] for p in part], axis=1), bias(n))

    blocks = [(lambda: kx_ref[0], lambda: vx_ref[0], None)]
    blocks += [latent_block(n) for n in range(n_parts // parts_per_block)]
    streams = [_HeadsFlash(qt_ref, s_scr, max_scr, head0, range(kv * GROUP, (kv + 1) * GROUP), sink_ref)
               for kv in range(2 * pairs)]
    slot0 = 0
    streams[0].score(slot0, blocks[0][0](), blocks[0][2])
    for n, flash in enumerate(streams):
        for u, (_, vt_blk, _) in enumerate(blocks):
            if u + 1 < len(blocks):
                flash.score((slot0 + u + 1) % 2, blocks[u + 1][0](), blocks[u + 1][2])
            elif n + 1 < len(streams):
                slot0 = 1 - (slot0 + u) % 2
                streams[n + 1].score(slot0, blocks[0][0](), blocks[0][2])
                flash.consume(1 - slot0, vt_blk())
                break
            flash.consume((slot0 + u) % 2, vt_blk())
        flash.finish(gzt_ref, o_ref)


def _local_attention(qt, k, vt, k_ctx, vt_ctx, gz, table, sink, *, part, lo_parts):
    b, _, t = qt.shape
    tq = Q_TILE
    nt = t // tq
    has_sink = sink is not None
    table_heads = table.shape[1]
    pairs = LOCAL_PAIRS_PER_STEP
    width = pairs * PAIR_WIDTH
    tab_blk_heads = pairs * PAIR_HEADS if table_heads > 1 else 1
    span = table.shape[2]
    n_parts = span // part
    parts_per_block = KEY_TILE // part
    assert span % KEY_TILE == 0 and KEY_TILE % part == 0 and tq % part == 0

    def cls(i):
        return jnp.where(i == 0, 0, jnp.where(i == nt - 1, 2, 1))

    def part_index(i, p):
        return jnp.clip(i * (tq // part) - lo_parts + p, 0, t // part - 1)

    in_specs = [pl.BlockSpec((1, width, tq), lambda j, i, bi: (bi, j, i))]
    in_specs += [pl.BlockSpec((1, part, pairs * LANES), lambda j, i, bi, p=p: (bi, part_index(i, p), j))
                 for p in range(n_parts)]
    in_specs += [pl.BlockSpec((1, k_ctx.shape[1], pairs * LANES), lambda j, i, bi: (bi, 0, j))]
    in_specs += [pl.BlockSpec((1, pairs * 2 * HEAD_DIM, part), lambda j, i, bi, p=p: (bi, j, part_index(i, p)))
                 for p in range(n_parts)]
    in_specs += [
        pl.BlockSpec((1, pairs * 2 * HEAD_DIM, vt_ctx.shape[2]), lambda j, i, bi: (bi, j, 0)),
        pl.BlockSpec((1, tab_blk_heads, span, tq),
                     lambda j, i, bi: (cls(i), j if table_heads > 1 else 0, 0, 0)),
        pl.BlockSpec((1, width, tq), lambda j, i, bi: (bi, j, i)),
    ]
    args = [qt] + [k] * n_parts + [k_ctx] + [vt] * n_parts + [vt_ctx, table, gz]
    if has_sink:
        in_specs = [pl.BlockSpec(memory_space=pltpu.SMEM)] + in_specs
        args = [sink] + args
    return pl.pallas_call(
        functools.partial(_local_attn_kernel, has_sink=has_sink, table_heads=table_heads,
                          n_parts=n_parts, parts_per_block=parts_per_block, pairs=pairs),
        grid=(N_KV_PAIRS // pairs, nt, b),
        in_specs=in_specs,
        out_specs=pl.BlockSpec((1, width, tq), lambda j, i, bi: (bi, j, i)),
        out_shape=jax.ShapeDtypeStruct((b, ATTN_WIDTH, t), BF16),
        scratch_shapes=[pltpu.VMEM((2, GROUP, KEY_TILE, tq), F32),
                        pltpu.VMEM((2, GROUP, 1, tq), F32)],
        compiler_params=pltpu.CompilerParams(vmem_limit_bytes=VMEM_LIMIT),
        name="local_attn_sink" if has_sink else "local_attn",
    )(*args)


def _post_kernel(ogt_ref, w_ref, h_ref, mod_ref, gpost_ref, o_ref):
    y = lax.dot_general(ogt_ref[0], w_ref[...], (((0,), (0,)), ((), ())), preferred_element_type=F32)
    ms = jnp.mean(y * y, axis=-1, keepdims=True)
    yn = y * lax.rsqrt(ms + EPS) * gpost_ref[...]
    o_ref[0] = h_ref[0] + mod_ref[0, 2:3, :] * yn


def _post(og, w_out, h, mod, g_post, *, shared_mod):
    b, t, _ = h.shape
    tt = min(POST_TILE, t)
    mod_map = (lambda bi, ti: (0, 0, 0)) if shared_mod else (lambda bi, ti: (bi, 0, 0))
    return pl.pallas_call(
        _post_kernel,
        grid=(b, t // tt),
        in_specs=[
            pl.BlockSpec((1, ATTN_WIDTH, tt), lambda bi, ti: (bi, 0, ti)),
            pl.BlockSpec((ATTN_WIDTH, D_MODEL), lambda bi, ti: (0, 0)),
            pl.BlockSpec((1, tt, D_MODEL), lambda bi, ti: (bi, ti, 0)),
            pl.BlockSpec((1, 3, D_MODEL), mod_map),
            pl.BlockSpec((1, D_MODEL), lambda bi, ti: (0, 0)),
        ],
        out_specs=pl.BlockSpec((1, tt, D_MODEL), lambda bi, ti: (bi, ti, 0)),
        out_shape=jax.ShapeDtypeStruct((b, t, D_MODEL), F32),
        compiler_params=pltpu.CompilerParams(vmem_limit_bytes=VMEM_LIMIT),
        name="post",
    )(og, w_out, h, mod, g_post)


def _rope_tables_t(n):
    t = jnp.arange(n, dtype=jnp.int32)
    row = (t // GRID_W).astype(F32)
    col = (t % GRID_W).astype(F32)
    inv = ROPE_BASE ** (-jnp.arange(ROPE_PAIRS, dtype=F32) / ROPE_PAIRS)
    ang = jnp.concatenate([row[:, None] * inv, col[:, None] * inv], axis=-1)
    return jnp.cos(ang).T, jnp.sin(ang).T


def _tile_positions(n, cls, lo, span):
    nt = n // Q_TILE
    assert nt >= 3 and lo <= Q_TILE and span - lo <= 2 * Q_TILE
    tile = {0: 0, 1: 1, 2: nt - 1}[cls]
    q_pos = tile * Q_TILE + np.arange(Q_TILE)
    k_pos = tile * Q_TILE - lo + np.arange(span)
    in_range = (k_pos >= 0) & (k_pos < n)
    return q_pos, k_pos, in_range


def _window_table(n):
    span = Q_TILE + 2 * WINDOW
    tabs = []
    for cls in range(3):
        q_pos, k_pos, in_range = _tile_positions(n, cls, WINDOW, span)
        valid = (np.abs(k_pos[:, None] - q_pos[None, :]) <= WINDOW) & in_range[:, None]
        assert (valid.sum(axis=0) == np.minimum(q_pos + WINDOW, n - 1) - np.maximum(q_pos - WINDOW, 0) + 1).all()
        tabs.append(np.where(valid, 0.0, NEG_INF).astype(np.float32)[None])
    return jnp.asarray(np.stack(tabs))


def _neighborhood_table(n, bias_table):
    rows = n // GRID_W
    wr = min(NA_MAX_ROWS, rows)
    wc = NA_COLS
    assert Q_TILE % GRID_W == 0
    tile_rows = Q_TILE // GRID_W
    dr_rel = (np.arange(3 * tile_rows)[:, None] - tile_rows - np.arange(tile_rows)[None, :]
              + NA_MAX_ROWS - 1)
    dc_rel = np.arange(GRID_W)[:, None] - np.arange(GRID_W)[None, :] + NA_COLS - 1
    oh_r = (dr_rel[..., None] == np.arange(2 * NA_MAX_ROWS - 1)).astype(np.float32)
    oh_c = (dc_rel[..., None] == np.arange(2 * NA_COLS - 1)).astype(np.float32)
    full = jnp.einsum('kqa,hab,cdb->hkcqd', oh_r, bias_table.astype(F32), oh_c,
                      precision=lax.Precision.HIGHEST)
    full = full.reshape(bias_table.shape[0], 3 * Q_TILE, Q_TILE)
    tabs = []
    for cls in range(3):
        q_pos, k_pos, in_range = _tile_positions(n, cls, Q_TILE, 3 * Q_TILE)
        k_pos = np.clip(k_pos, 0, n - 1)
        r, c_ = q_pos // GRID_W, q_pos % GRID_W
        kr, kc = k_pos // GRID_W, k_pos % GRID_W
        rs = np.clip(r - wr // 2, 0, rows - wr)
        cs = np.clip(c_ - wc // 2, 0, GRID_W - wc)
        valid = ((kr[:, None] >= rs[None, :]) & (kr[:, None] < rs[None, :] + wr)
                 & (kc[:, None] >= cs[None, :]) & (kc[:, None] < cs[None, :] + wc)
                 & in_range[:, None])
        assert (valid.sum(axis=0) == wr * wc).all()
        dr = kr[:, None] - r[None, :] + NA_MAX_ROWS - 1
        dc = kc[:, None] - c_[None, :] + NA_COLS - 1
        dr_tab = np.broadcast_to(dr_rel[:, None, :, None], (3 * tile_rows, GRID_W, tile_rows, GRID_W))
        dc_tab = np.broadcast_to(dc_rel[None, :, None, :], (3 * tile_rows, GRID_W, tile_rows, GRID_W))
        assert (dr_tab.reshape(dr.shape)[valid] == dr[valid]).all()
        assert (dc_tab.reshape(dc.shape)[valid] == dc[valid]).all()
        tabs.append(jnp.where(valid[None], full, NEG_INF))
    return jnp.stack(tabs)


def kernel(x_prompt, x_sample, cache_k, cache_v, c, c_ctx, w_mod, b_mod, norm_pre, norm_post,
           w_in, q_norm, k_norm, w_out, sink_logit, na_rel_bias):
    depth = w_mod.shape[0]
    n_lat = x_sample.shape[1]
    dec_batch = x_sample.shape[0]
    assert dec_batch + 1 <= COND_ROWS
    assert n_lat % Q_TILE == 0 and x_prompt.shape[1] % Q_TILE == 0 and cache_k.shape[2] % KEY_TILE == 0

    cond = jnp.zeros((COND_ROWS, D_MODEL), F32).at[:dec_batch].set(c).at[dec_batch].set(c_ctx)
    mod_all = _modulation(cond, w_mod, b_mod).reshape(depth, COND_ROWS, 3, D_MODEL)

    cos_t, sin_t = _rope_tables_t(n_lat)
    scale = HEAD_DIM ** -0.5
    w_in_b = w_in.astype(BF16)
    w_out_b = w_out.astype(BF16)
    k_ctx_all = cache_k.reshape(dec_batch, depth, -1, KV_WIDTH).astype(BF16)
    vt_ctx_all = cache_v.reshape(dec_batch, depth, -1, KV_WIDTH).transpose(0, 1, 3, 2).astype(BF16)
    win_table = _window_table(n_lat)

    h_ctx, h_lat = x_prompt, x_sample
    new_k, new_v = [], []
    for l in range(depth):
        kind = l % N_MIXERS
        sink = sink_logit[l // N_MIXERS] if kind == 1 else None
        g_pre = norm_pre[l][None]
        g_post = norm_post[l][None]
        qg = jnp.broadcast_to((q_norm[l] * (scale * LOG2E))[:, None], (HEAD_DIM, TOKEN_TILE))
        kg = jnp.broadcast_to(k_norm[l][:, None], (HEAD_DIM, TOKEN_TILE))
        mod_ctx = mod_all[l, dec_batch:dec_batch + 1]
        mod_lat = mod_all[l, :dec_batch]

        qt, k, vt, gz, kf, vf = _pre(h_ctx, mod_ctx, g_pre, w_in_b[l], qg, kg, None, None,
                                     shared_mod=True, emit_kv=True)
        new_k.append(kf.reshape(kf.shape[0], kf.shape[1], N_KV_HEADS, HEAD_DIM))
        new_v.append(vf.reshape(vf.shape[0], vf.shape[1], N_KV_HEADS, HEAD_DIM))
        og = _dense_attention(qt, k, vt, gz, sink)
        h_ctx = _post(og, w_out_b[l], h_ctx, mod_ctx, g_post, shared_mod=True)

        use_rope = kind != 2
        qt, k, vt, gz = _pre(h_lat, mod_lat, g_pre, w_in_b[l], qg, kg,
                             cos_t if use_rope else None, sin_t if use_rope else None,
                             shared_mod=False, emit_kv=False)
        k_ctx, vt_ctx = k_ctx_all[:, l], vt_ctx_all[:, l]
        if kind == 0:
            og = _dense_attention(qt, jnp.concatenate([k, k_ctx], axis=1),
                                  jnp.concatenate([vt, vt_ctx], axis=2), gz, None)
        elif kind == 1:
            og = _local_attention(qt, k, vt, k_ctx, vt_ctx, gz, win_table, sink, part=WINDOW, lo_parts=1)
        else:
            table = _neighborhood_table(n_lat, na_rel_bias[l // N_MIXERS] * LOG2E)
            og = _local_attention(qt, k, vt, k_ctx, vt_ctx, gz, table, None, part=Q_TILE, lo_parts=1)
        h_lat = _post(og, w_out_b[l], h_lat, mod_lat, g_post, shared_mod=False)

    return (h_ctx, h_lat, jnp.stack(new_k, axis=1), jnp.stack(new_v, axis=1))
```
